```python
import jax, jax.numpy as jnp
from jax import lax
import numpy as np

D_MODEL = 2048
BATCH = 8
SEQ = 2048
DEPTH = 4

CHUNK = 64
MEM_LEN = 256
BRANCH_W = 1024
N_BRANCH = 5
A_HEADS = 16
A_KV_HEADS = 2
A_HD = 64
A_WINDOW = 128
A_NWC = A_WINDOW // CHUNK
SC_WIDTH = 3
CF_WIDTH = 31
ML_HEADS = 4
ML_DK = 128
ML_DV = 256
XA_HEADS = 4
XA_HD = BRANCH_W // XA_HEADS
DN_ALPHA = (2 * DEPTH) ** 0.25
DN_BETA = (8 * DEPTH) ** -0.25
LN_EPS = 1e-5

IN_SPLITS = (
    A_HEADS * A_HD, A_KV_HEADS * A_HD, A_KV_HEADS * A_HD,
    BRANCH_W, BRANCH_W, BRANCH_W,
    BRANCH_W, BRANCH_W,
    ML_HEADS * ML_DK, ML_HEADS * ML_DK, ML_HEADS * ML_DV,
    ML_HEADS, ML_HEADS, BRANCH_W,
    BRANCH_W,
    N_BRANCH * BRANCH_W,
)
N_IN = sum(IN_SPLITS)
IN_OFFSETS = tuple(int(v) for v in np.cumsum(IN_SPLITS)[:-1])

kernel_name = 'hybrid_gated_streaming_encoder'


def layer_norm(x, g, b):
    xf = x.astype(jnp.float32)
    mu = xf.mean(-1, keepdims=True)
    var = jnp.square(xf - mu).mean(-1, keepdims=True)
    y = (xf - mu) * lax.rsqrt(var + LN_EPS) * g.astype(jnp.float32) + b.astype(jnp.float32)
    return y.astype(x.dtype)


def causal_dwconv(u, w):
    width, ch = w.shape
    return lax.conv_general_dilated(
        u, w[:, None, :].astype(u.dtype), (1,), [(width - 1, 0)],
        dimension_numbers=('NWC', 'WIO', 'NWC'), feature_group_count=ch)


def alibi_slopes(n):
    return np.array([2.0 ** (-8.0 * (h + 1) / n) for h in range(n)], dtype=np.float32)


def window_sink_attention(q, k, v, sinks):
    bsz, seq, _ = q.shape
    n_chunks = seq // CHUNK
    grp = A_HEADS // A_KV_HEADS
    band = (A_NWC + 1) * CHUNK
    q = q.reshape(bsz, n_chunks, CHUNK, A_KV_HEADS, grp, A_HD)
    k = k.reshape(bsz, n_chunks, CHUNK, A_KV_HEADS, A_HD)
    v = v.reshape(bsz, n_chunks, CHUNK, A_KV_HEADS, A_HD)
    pad = ((0, 0), (A_NWC, 0), (0, 0), (0, 0), (0, 0))
    kp, vp = jnp.pad(k, pad), jnp.pad(v, pad)
    kb = jnp.concatenate([kp[:, j:j + n_chunks] for j in range(A_NWC + 1)], axis=2)
    vb = jnp.concatenate([vp[:, j:j + n_chunks] for j in range(A_NWC + 1)], axis=2)
    s = jnp.einsum('bcqkgd,bcskd->bckgqs', q, kb,
                   preferred_element_type=jnp.float32) * (A_HD ** -0.5)
    t_pos = np.arange(CHUNK)[:, None] + A_NWC * CHUNK
    s_pos = np.arange(band)[None, :]
    dist = jnp.asarray(np.abs(t_pos - s_pos).astype(np.float32))
    slopes = jnp.asarray(alibi_slopes(A_HEADS)).reshape(A_KV_HEADS, grp)
    s = s - slopes[:, :, None, None] * dist
    valid = (jnp.arange(n_chunks)[:, None] - A_NWC + jnp.arange(band)[None, :] // CHUNK) >= 0
    s = jnp.where(valid[None, :, None, None, None, :], s, -jnp.inf)
    snk = sinks.astype(jnp.float32).reshape(A_KV_HEADS, grp)[:, :, None, None]
    mx = jnp.maximum(s.max(-1, keepdims=True), snk)
    p = jnp.exp(s - mx)
    p = p / (p.sum(-1, keepdims=True) + jnp.exp(snk - mx))
    o = jnp.einsum('bckgqs,bcskd->bcqkgd', p.astype(v.dtype), vb)
    return o.reshape(bsz, seq, A_HEADS * A_HD)


def mlstm_chunkwise(q, k, v, ig, lf):
    bsz, seq = q.shape[:2]
    n_chunks = seq // CHUNK

    def heads_chunks(a):
        return a.reshape(bsz, n_chunks, CHUNK, ML_HEADS, -1).transpose(1, 0, 3, 2, 4)

    def gate_chunks(a):
        return a.reshape(bsz, n_chunks, CHUNK, ML_HEADS).transpose(1, 0, 3, 2)

    causal = jnp.tril(jnp.ones((CHUNK, CHUNK), dtype=bool))

    def step(carry, inp):
        c_st, n_st, m_st = carry
        qc, kc, vc, ic, fc = inp
        b = jnp.cumsum(fc, axis=-1)
        dmat = jnp.where(causal, b[..., :, None] - b[..., None, :] + ic[..., None, :], -jnp.inf)
        inter = b + m_st[..., None]
        m_t = jnp.maximum(inter, dmat.max(-1))
        w_intra = jnp.exp(dmat - m_t[..., None])
        w_inter = jnp.exp(inter - m_t)
        qk = jnp.einsum('bhtd,bhsd->bhts', qc, kc) * w_intra
        num = (jnp.einsum('bhts,bhsv->bhtv', qk, vc)
               + w_inter[..., None] * jnp.einsum('bhtd,bhdv->bhtv', qc, c_st))
        den = qk.sum(-1) + w_inter * jnp.einsum('bhtd,bhd->bht', qc, n_st)
        h = num / jnp.maximum(jnp.abs(den), jnp.exp(-m_t))[..., None]
        m_new = m_t[..., -1]
        w_state = jnp.exp(b[..., -1:] - b + ic - m_new[..., None])
        decay = jnp.exp(b[..., -1] + m_st - m_new)
        c_new = decay[..., None, None] * c_st + jnp.einsum('bhs,bhsd,bhsv->bhdv', w_state, kc, vc)
        n_new = decay[..., None] * n_st + jnp.einsum('bhs,bhsd->bhd', w_state, kc)
        return (c_new, n_new, m_new), h

    init = (jnp.zeros((bsz, ML_HEADS, ML_DK, ML_DV), jnp.float32),
            jnp.zeros((bsz, ML_HEADS, ML_DK), jnp.float32),
            jnp.zeros((bsz, ML_HEADS), jnp.float32))
    _, h = lax.scan(step, init, (heads_chunks(q), heads_chunks(k), heads_chunks(v),
                                 gate_chunks(ig), gate_chunks(lf)))
    return h.transpose(1, 0, 3, 2, 4).reshape(bsz, seq, ML_HEADS * ML_DV)


def memory_cross_attention(q, mem, wkv):
    bsz, seq, _ = q.shape
    kv = mem @ wkv
    k, v = jnp.split(kv, 2, axis=-1)
    q = q.reshape(bsz, seq, XA_HEADS, XA_HD)
    k = k.reshape(bsz, -1, XA_HEADS, XA_HD)
    v = v.reshape(bsz, -1, XA_HEADS, XA_HD)
    s = jnp.einsum('bshd,bmhd->bhsm', q, k, preferred_element_type=jnp.float32) * (XA_HD ** -0.5)
    p = jax.nn.softmax(s, axis=-1)
    o = jnp.einsum('bhsm,bmhd->bshd', p.astype(v.dtype), v)
    return o.reshape(bsz, seq, BRANCH_W)


def hybrid_layer(x, mem, w_in, w_gate, w_bout, w_out, sinks, sc_w, cf_w, cf_g, cf_b,
                 ml_ib, ml_fb, mem_wkv, ln_g, ln_b):
    bsz, seq, _ = x.shape
    proj = x @ w_in
    (a_q, a_k, a_v, sc_h, sc_bg, sc_cg, cf_val, cf_gt, ml_q, ml_k, ml_v,
     ml_i, ml_f, ml_o, xa_q, silu_paths) = jnp.split(proj, IN_OFFSETS, axis=-1)

    y_a = window_sink_attention(a_q, a_k, a_v, sinks)
    y_b = sc_bg * causal_dwconv(sc_cg * sc_h, sc_w)
    u = cf_val * jax.nn.sigmoid(cf_gt)
    y_c = jax.nn.silu(layer_norm(causal_dwconv(u, cf_w), cf_g, cf_b))
    f32 = jnp.float32
    h = mlstm_chunkwise(
        ml_q.reshape(bsz, seq, ML_HEADS, ML_DK).astype(f32),
        ml_k.reshape(bsz, seq, ML_HEADS, ML_DK).astype(f32) * (ML_DK ** -0.5),
        ml_v.reshape(bsz, seq, ML_HEADS, ML_DV).astype(f32),
        (ml_i + ml_ib).astype(f32),
        jax.nn.log_sigmoid((ml_f + ml_fb).astype(f32)))
    y_d = jax.nn.sigmoid(ml_o) * h.astype(x.dtype)
    y_e = memory_cross_attention(xa_q, mem, mem_wkv)

    paths = jnp.split(silu_paths, N_BRANCH, axis=-1)
    merged = jnp.zeros_like(x)
    for n, y_n in enumerate((y_a, y_b, y_c, y_d, y_e)):
        z = y_n * jax.nn.silu(paths[n])
        merged = merged + jax.nn.sigmoid(x @ w_gate[n]) * (z @ w_bout[n])
    out = merged @ w_out
    return layer_norm(DN_ALPHA * x + out, ln_g, ln_b)


def setup_inputs(seed: int = 0) -> dict:
    key = jax.random.key(seed)
    ks = jax.random.split(key, 20)
    nrm = jax.random.normal
    f32 = jnp.float32
    d, w = D_MODEL, BRANCH_W
    return {
        'x': nrm(ks[0], (BATCH, SEQ, d), f32),
        'mem': nrm(ks[1], (BATCH, MEM_LEN, d), f32),
        'ln_in_g': 1.0 + 0.02 * nrm(ks[2], (d,), f32),
        'ln_in_b': 0.02 * nrm(ks[3], (d,), f32),
        'w_in': nrm(ks[4], (DEPTH, d, N_IN), f32) * d ** -0.5,
        'w_gate': nrm(ks[5], (DEPTH, N_BRANCH, d, d), f32) * d ** -0.5,
        'w_bout': nrm(ks[6], (DEPTH, N_BRANCH, w, d), f32) * (DN_BETA * w ** -0.5),
        'w_out': nrm(ks[7], (DEPTH, d, d), f32) * (DN_BETA * d ** -0.5),
        'attn_sinks': 0.5 * nrm(ks[8], (DEPTH, A_HEADS), f32),
        'sc_w': nrm(ks[9], (DEPTH, SC_WIDTH, w), f32) * SC_WIDTH ** -0.5,
        'cf_w': nrm(ks[10], (DEPTH, CF_WIDTH, w), f32) * CF_WIDTH ** -0.5,
        'cf_ln_g': 1.0 + 0.02 * nrm(ks[11], (DEPTH, w), f32),
        'cf_ln_b': 0.02 * nrm(ks[12], (DEPTH, w), f32),
        'ml_i_bias': 0.1 * nrm(ks[13], (DEPTH, ML_HEADS), f32),
        'ml_f_bias': jnp.linspace(3.0, 6.0, ML_HEADS, dtype=f32)[None, :]
                     + 0.1 * nrm(ks[14], (DEPTH, ML_HEADS), f32),
        'mem_wkv': nrm(ks[15], (DEPTH, d, 2 * w), f32) * d ** -0.5,
        'ln_g': 1.0 + 0.02 * nrm(ks[16], (DEPTH, d), f32),
        'ln_b': 0.02 * nrm(ks[17], (DEPTH, d), f32),
    }


def reference(x, mem, ln_in_g, ln_in_b, w_in, w_gate, w_bout, w_out, attn_sinks, sc_w, cf_w,
              cf_ln_g, cf_ln_b, ml_i_bias, ml_f_bias, mem_wkv, ln_g, ln_b):
    h = layer_norm(x, ln_in_g, ln_in_b)
    for l in range(DEPTH):
        h = hybrid_layer(h, mem, w_in[l], w_gate[l], w_bout[l], w_out[l], attn_sinks[l],
                         sc_w[l], cf_w[l], cf_ln_g[l], cf_ln_b[l], ml_i_bias[l], ml_f_bias[l],
                         mem_wkv[l], ln_g[l], ln_b[l])
    return h
```

```python
import functools

import jax
import jax.numpy as jnp
import numpy as np
from jax import lax
from jax.experimental import pallas as pl
from jax.experimental.pallas import tpu as pltpu

F32 = jnp.float32
BF16 = jnp.bfloat16

D_MODEL = 2048
DEPTH = 4
CHUNK = 64
BRANCH_W = 1024
N_BRANCH = 5
A_HEADS = 16
A_KV_HEADS = 2
A_HD = 64
A_WINDOW = 128
SC_WIDTH = 3
CF_WIDTH = 31
ML_HEADS = 4
ML_DK = 128
ML_DV = 256
XA_HEADS = 4
XA_HD = BRANCH_W // XA_HEADS
DN_ALPHA = (2 * DEPTH) ** 0.25
LN_EPS = 1e-5

IN_SPLITS = (
    A_HEADS * A_HD, A_KV_HEADS * A_HD, A_KV_HEADS * A_HD,
    BRANCH_W, BRANCH_W, BRANCH_W,
    BRANCH_W, BRANCH_W,
    ML_HEADS * ML_DK, ML_HEADS * ML_DK, ML_HEADS * ML_DV,
    ML_HEADS, ML_HEADS, BRANCH_W,
    BRANCH_W,
    N_BRANCH * BRANCH_W,
)
_OFF = tuple(int(v) for v in np.cumsum((0,) + IN_SPLITS))
(O_AQ, O_AK, O_AV, O_SCH, O_SCBG, O_SCCG, O_CFV, O_CFG, O_MLQ, O_MLK, O_MLV,
 O_MLI, O_MLF, O_MLO, O_XAQ, O_SILU, O_END) = _OFF

LANES = 128
HALO = 32
ALIBI_SLOPES = tuple(float(2.0 ** (-8.0 * (h + 1) / A_HEADS)) for h in range(A_HEADS))

TM_MM = 1024
TQ_ATTN = 256
TS_CONV = 256
RC_CONV = 64
L_MLSTM = 128
TQ_XA = 512
TM_MERGE = 512
TN_MERGE = 256
TM_OUT = 512
VMEM_LIMIT = 56 * 1024 * 1024


def _cparams(sem):
    return pltpu.CompilerParams(dimension_semantics=sem, vmem_limit_bytes=VMEM_LIMIT)


def _sigmoid(x):
    return 1.0 / (1.0 + jnp.exp(-x))


def _silu(x):
    return x * _sigmoid(x)


def _ln_rows(x, g, b):
    mu = jnp.mean(x, axis=-1, keepdims=True)
    xc = x - mu
    var = jnp.mean(xc * xc, axis=-1, keepdims=True)
    return xc * lax.rsqrt(var + LN_EPS) * g + b


def _ln_kernel(x_ref, g_ref, b_ref, of_ref, ob_ref):
    y = _ln_rows(x_ref[...], g_ref[...], b_ref[...])
    of_ref[...] = y
    ob_ref[...] = y.astype(BF16)


def layer_norm_in(x2d, g, b, tm=512):
    t, d = x2d.shape
    return pl.pallas_call(
        _ln_kernel,
        grid=(t // tm,),
        in_specs=[pl.BlockSpec((tm, d), lambda i: (i, 0)),
                  pl.BlockSpec((1, d), lambda i: (0, 0)),
                  pl.BlockSpec((1, d), lambda i: (0, 0))],
        out_specs=[pl.BlockSpec((tm, d), lambda i: (i, 0)),
                   pl.BlockSpec((tm, d), lambda i: (i, 0))],
        out_shape=[jax.ShapeDtypeStruct((t, d), F32), jax.ShapeDtypeStruct((t, d), BF16)],
        compiler_params=_cparams(("parallel",)),
        name="ln_in",
    )(x2d, g.reshape(1, d), b.reshape(1, d))


def _mm_kernel(a_ref, b_ref, o_ref):
    o_ref[...] = jnp.dot(a_ref[...], b_ref[...], preferred_element_type=F32).astype(o_ref.dtype)


def matmul(a, b, out_dtype, tm, tn, name):
    m, k = a.shape
    n = b.shape[1]
    tm = min(tm, m)
    return pl.pallas_call(
        _mm_kernel,
        grid=(m // tm, n // tn),
        in_specs=[pl.BlockSpec((tm, k), lambda i, j: (i, 0)),
                  pl.BlockSpec((k, tn), lambda i, j: (0, j))],
        out_specs=pl.BlockSpec((tm, tn), lambda i, j: (i, j)),
        out_shape=jax.ShapeDtypeStruct((m, n), out_dtype),
        compiler_params=_cparams(("parallel", "parallel")),
        name=name,
    )(a, b)


def _attn_kernel(q_ref, kvc_ref, kvp_ref, path_ref, snk_ref, o_ref, *, tq):
    i = pl.program_id(1)
    nk = A_WINDOW + tq
    kv = jnp.concatenate([kvp_ref[...], kvc_ref[...]], axis=0)
    t_pos = lax.broadcasted_iota(jnp.int32, (tq, nk), 0)
    s_pos = lax.broadcasted_iota(jnp.int32, (tq, nk), 1) - A_WINDOW
    q_chunk = t_pos // CHUNK
    k_chunk = (s_pos + A_WINDOW) // CHUNK - A_WINDOW // CHUNK
    in_band = jnp.logical_and(k_chunk <= q_chunk, k_chunk >= q_chunk - A_WINDOW // CHUNK)
    exists = jnp.logical_or(s_pos >= 0, i > 0)
    dist = jnp.abs(t_pos - s_pos).astype(F32)
    neg_dist = jnp.where(jnp.logical_and(in_band, exists), -dist, -jnp.inf)
    grp = A_HEADS // A_KV_HEADS
    scale = A_HD ** -0.5
    for hp in range(A_HEADS // 2):
        pair = []
        for h in (2 * hp, 2 * hp + 1):
            g = h // grp
            qh = q_ref[:, h * A_HD:(h + 1) * A_HD]
            kg = kv[:, g * A_HD:(g + 1) * A_HD]
            vg = kv[:, A_KV_HEADS * A_HD + g * A_HD:A_KV_HEADS * A_HD + (g + 1) * A_HD]
            s = lax.dot_general(qh, kg, (((1,), (1,)), ((), ())), preferred_element_type=F32)
            s = s * scale + ALIBI_SLOPES[h] * neg_dist
            snk = snk_ref[h:h + 1, 0:1]
            mx = jnp.maximum(jnp.max(s, axis=-1, keepdims=True), snk)
            p = jnp.exp(s - mx)
            den = jnp.sum(p, axis=-1, keepdims=True) + jnp.exp(snk - mx)
            o = jnp.dot(p.astype(BF16), vg, preferred_element_type=F32)
            pair.append(o / den)
        y = jnp.concatenate(pair, axis=1)
        path = path_ref[:, hp * LANES:(hp + 1) * LANES].astype(F32)
        o_ref[:, hp * LANES:(hp + 1) * LANES] = (y * _silu(path)).astype(BF16)


def window_attention(qkv, silu, sinks, bsz, seq, tq=TQ_ATTN):
    t = bsz * seq
    nq = seq // tq
    kvw = 2 * A_KV_HEADS * A_HD
    kv_col = (A_HEADS * A_HD) // kvw
    snk = jnp.broadcast_to(sinks.astype(F32)[:, None], (A_HEADS, LANES))
    return pl.pallas_call(
        functools.partial(_attn_kernel, tq=tq),
        grid=(bsz, nq),
        in_specs=[
            pl.BlockSpec((tq, A_HEADS * A_HD), lambda b, i: (b * nq + i, 0)),
            pl.BlockSpec((tq, kvw), lambda b, i: (b * nq + i, kv_col)),
            pl.BlockSpec((A_WINDOW, kvw),
                         lambda b, i: (jnp.maximum((b * nq + i) * (tq // A_WINDOW) - 1, 0), kv_col)),
            pl.BlockSpec((tq, BRANCH_W), lambda b, i: (b * nq + i, 0)),
            pl.BlockSpec((A_HEADS, LANES), lambda b, i: (0, 0)),
        ],
        out_specs=pl.BlockSpec((tq, BRANCH_W), lambda b, i: (b * nq + i, 0)),
        out_shape=jax.ShapeDtypeStruct((t, BRANCH_W), BF16),
        compiler_params=_cparams(("parallel", "parallel")),
        name="window_attn",
    )(qkv, qkv, qkv, silu, snk)


def _conv_kernel(h_ref, bg_ref, cg_ref, val_ref, gt_ref, hp_ref, cgp_ref, valp_ref, gtp_ref,
                 pb_ref, pc_ref, scw_ref, cfw_ref, lng_ref, lnb_ref, zb_ref, zc_ref,
                 ub_scr, uc_scr, cv_scr, *, ts, rc):
    i = pl.program_id(1)
    first = i == 0

    for s in range(BRANCH_W // LANES):
        ln = slice(s * LANES, (s + 1) * LANES)
        hist_b = cgp_ref[:, ln].astype(F32) * hp_ref[:, ln].astype(F32)
        hist_c = valp_ref[:, ln].astype(F32) * _sigmoid(gtp_ref[:, ln].astype(F32))
        ub_scr[s, 0:HALO, :] = jnp.where(first, 0.0, hist_b)
        uc_scr[s, 0:HALO, :] = jnp.where(first, 0.0, hist_c)

        def fill(r, carry, s=s, ln=ln):
            r0 = pl.multiple_of(r * rc, rc)
            rows = pl.ds(r0, rc)
            dst = pl.ds(r0 + HALO, rc)
            ub_scr[s, dst, :] = cg_ref[rows, ln].astype(F32) * h_ref[rows, ln].astype(F32)
            uc_scr[s, dst, :] = val_ref[rows, ln].astype(F32) * _sigmoid(gt_ref[rows, ln].astype(F32))
            return carry

        lax.fori_loop(0, ts // rc, fill, 0)

        def taps(r, carry, s=s, ln=ln):
            r0 = pl.multiple_of(r * rc, rc)
            rows = pl.ds(r0, rc)
            acc = jnp.zeros((rc, LANES), F32)
            for j in range(SC_WIDTH):
                win = ub_scr[s, pl.ds(r0 + (HALO - (SC_WIDTH - 1) + j), rc, stride=1), :]
                acc = acc + scw_ref[j:j + 1, ln] * win
            yb = bg_ref[rows, ln].astype(F32) * acc
            zb_ref[rows, ln] = (yb * _silu(pb_ref[rows, ln].astype(F32))).astype(BF16)
            acc = jnp.zeros((rc, LANES), F32)
            for j in range(CF_WIDTH):
                win = uc_scr[s, pl.ds(r0 + (HALO - (CF_WIDTH - 1) + j), rc, stride=1), :]
                acc = acc + cfw_ref[j:j + 1, ln] * win
            cv_scr[rows, ln] = acc
            return carry

        lax.fori_loop(0, ts // rc, taps, 0)

    def norm(r, carry):
        rows = pl.ds(pl.multiple_of(r * rc, rc), rc)
        yc = _silu(_ln_rows(cv_scr[rows, :], lng_ref[...], lnb_ref[...]))
        zc_ref[rows, :] = (yc * _silu(pc_ref[rows, :].astype(F32))).astype(BF16)
        return carry

    lax.fori_loop(0, ts // rc, norm, 0)


def conv_mixers(conv, silu, sc_w, cf_w, cf_g, cf_b, bsz, seq, ts=TS_CONV, rc=RC_CONV):
    t = bsz * seq
    ns = seq // ts
    w = BRANCH_W

    def cur(col):
        return pl.BlockSpec((ts, w), lambda b, i: (b * ns + i, col))

    def prev(col):
        return pl.BlockSpec((HALO, w),
                            lambda b, i: (jnp.maximum((b * ns + i) * (ts // HALO) - 1, 0), col))

    def full(shape):
        return pl.BlockSpec(shape, lambda b, i: (0, 0))

    out_spec = pl.BlockSpec((ts, w), lambda b, i: (b * ns + i, 0))
    return pl.pallas_call(
        functools.partial(_conv_kernel, ts=ts, rc=rc),
        grid=(bsz, ns),
        in_specs=[cur(0), cur(1), cur(2), cur(3), cur(4), prev(0), prev(2), prev(3), prev(4),
                  cur(1), cur(2),
                  full((SC_WIDTH, w)), full((CF_WIDTH, w)), full((1, w)), full((1, w))],
        out_specs=[out_spec, out_spec],
        out_shape=[jax.ShapeDtypeStruct((t, w), BF16), jax.ShapeDtypeStruct((t, w), BF16)],
        scratch_shapes=[pltpu.VMEM((w // LANES, HALO + ts, LANES), F32),
                        pltpu.VMEM((w // LANES, HALO + ts, LANES), F32),
                        pltpu.VMEM((ts, w), F32)],
        compiler_params=_cparams(("parallel", "parallel")),
        name="conv_mixers",
    )(conv, conv, conv, conv, conv, conv, conv, conv, conv, silu, silu,
      sc_w.astype(F32), cf_w.astype(F32), cf_g.reshape(1, w).astype(F32), cf_b.reshape(1, w).astype(F32))


def _log_sigmoid(x):
    return jnp.minimum(x, 0.0) - jnp.log(1.0 + jnp.exp(-jnp.abs(x)))


def _mlstm_kernel(q_ref, k_ref, v_ref, o_ref, g_ref, bias_ref, path_ref, z_ref, c_scr, m_scr, *, lc):
    c = pl.program_id(1)

    @pl.when(c == 0)
    def _():
        c_scr[...] = jnp.zeros_like(c_scr)
        m_scr[...] = jnp.zeros_like(m_scr)

    gates = g_ref[...] + bias_ref[...]
    logf = _log_sigmoid(gates)
    row = lax.broadcasted_iota(jnp.int32, (lc, lc), 0)
    col = lax.broadcasted_iota(jnp.int32, (lc, lc), 1)
    causal = col <= row
    bcum = jnp.dot(causal.astype(F32), logf, preferred_element_type=F32,
                   precision=lax.Precision.HIGHEST)
    gates_t = gates.T
    bcum_t = bcum.T
    ones = jnp.ones((lc, LANES), BF16)
    scale = ML_DK ** -0.5
    for h in range(ML_HEADS):
        fl = ML_HEADS + h
        b_col = bcum[:, fl:fl + 1]
        b_row = bcum_t[fl:fl + 1, :]
        i_col = gates[:, h:h + 1]
        i_row = gates_t[h:h + 1, :]
        m_prev = m_scr[h:h + 1, 0:1]
        dmat = jnp.where(causal, b_col - b_row + i_row, -jnp.inf)
        inter = b_col + m_prev
        m_t = jnp.maximum(inter, jnp.max(dmat, axis=-1, keepdims=True))
        w_intra = jnp.exp(dmat - m_t)
        w_inter = jnp.exp(inter - m_t)
        qh = q_ref[:, h * ML_DK:(h + 1) * ML_DK]
        kh = k_ref[:, h * ML_DK:(h + 1) * ML_DK]
        v_ext = jnp.concatenate([v_ref[:, h * ML_DV:(h + 1) * ML_DV], ones], axis=1)
        qk = lax.dot_general(qh, kh, (((1,), (1,)), ((), ())), preferred_element_type=F32)
        qk = qk * scale * w_intra
        state = c_scr[h]
        num_ext = (jnp.dot(qk.astype(BF16), v_ext, preferred_element_type=F32)
                   + w_inter * jnp.dot(qh, state.astype(BF16), preferred_element_type=F32))
        num = num_ext[:, :ML_DV]
        den = num_ext[:, ML_DV:]
        den = jnp.maximum(jnp.abs(den), jnp.exp(-m_t))
        hh = num / jnp.concatenate([den, den], axis=1)
        b_last = bcum[lc - 1:lc, fl:fl + 1]
        m_new = m_t[lc - 1:lc, :]
        w_state = jnp.exp(b_last - b_col + i_col - m_new)
        decay = jnp.exp(b_last + m_prev - m_new)
        kw_t = (kh.astype(F32) * (scale * w_state)).T.astype(BF16)
        c_scr[h] = decay * state + jnp.dot(kw_t, v_ext, preferred_element_type=F32)
        m_scr[h:h + 1, :] = jnp.broadcast_to(m_new, (1, LANES))
        og = o_ref[:, h * ML_DV:(h + 1) * ML_DV].astype(F32)
        path = path_ref[:, h * ML_DV:(h + 1) * ML_DV].astype(F32)
        z_ref[:, h * ML_DV:(h + 1) * ML_DV] = (_sigmoid(og) * hh * _silu(path)).astype(BF16)


def mlstm(ml, gates, bias_row, silu, bsz, seq, lc=L_MLSTM):
    t = bsz * seq
    nc = seq // lc
    qw = ML_HEADS * ML_DK
    vw = ML_HEADS * ML_DV
    return pl.pallas_call(
        functools.partial(_mlstm_kernel, lc=lc),
        grid=(bsz, nc),
        in_specs=[
            pl.BlockSpec((lc, qw), lambda b, c: (b * nc + c, 0)),
            pl.BlockSpec((lc, qw), lambda b, c: (b * nc + c, 1)),
            pl.BlockSpec((lc, vw), lambda b, c: (b * nc + c, 1)),
            pl.BlockSpec((lc, vw), lambda b, c: (b * nc + c, 2)),
            pl.BlockSpec((lc, LANES), lambda b, c: (b * nc + c, 0)),
            pl.BlockSpec((1, LANES), lambda b, c: (0, 0)),
            pl.BlockSpec((lc, BRANCH_W), lambda b, c: (b * nc + c, 3)),
        ],
        out_specs=pl.BlockSpec((lc, BRANCH_W), lambda b, c: (b * nc + c, 0)),
        out_shape=jax.ShapeDtypeStruct((t, BRANCH_W), BF16),
        scratch_shapes=[pltpu.VMEM((ML_HEADS, ML_DK, ML_DV + LANES), F32),
                        pltpu.VMEM((8, LANES), F32)],
        compiler_params=_cparams(("parallel", "arbitrary")),
        name="mlstm",
    )(ml, ml, ml, ml, gates, bias_row, silu)


def _xattn_kernel(q_ref, kv_ref, path_ref, o_ref):
    scale = XA_HD ** -0.5
    for h in range(XA_HEADS):
        sl = slice(h * XA_HD, (h + 1) * XA_HD)
        kh = kv_ref[:, sl]
        vh = kv_ref[:, BRANCH_W + h * XA_HD:BRANCH_W + (h + 1) * XA_HD]
        s = lax.dot_general(q_ref[:, sl], kh, (((1,), (1,)), ((), ())),
                            preferred_element_type=F32) * scale
        mx = jnp.max(s, axis=-1, keepdims=True)
        p = jnp.exp(s - mx)
        den = jnp.sum(p, axis=-1, keepdims=True)
        o = jnp.dot(p.astype(BF16), vh, preferred_element_type=F32) / den
        o_ref[:, sl] = (o * _silu(path_ref[:, sl].astype(F32))).astype(BF16)


def cross_attention(xq, kv, silu, bsz, seq, mem_len, tq=TQ_XA):
    t = bsz * seq
    nq = seq // tq
    return pl.pallas_call(
        _xattn_kernel,
        grid=(bsz, nq),
        in_specs=[
            pl.BlockSpec((tq, BRANCH_W), lambda b, i: (b * nq + i, 0)),
            pl.BlockSpec((mem_len, 2 * BRANCH_W), lambda b, i: (b, 0)),
            pl.BlockSpec((tq, BRANCH_W), lambda b, i: (b * nq + i, 4)),
        ],
        out_specs=pl.BlockSpec((tq, BRANCH_W), lambda b, i: (b * nq + i, 0)),
        out_shape=jax.ShapeDtypeStruct((t, BRANCH_W), BF16),
        compiler_params=_cparams(("parallel", "parallel")),
        name="cross_attn",
    )(xq, kv, silu)


def _merge_kernel(x_ref, za_ref, zb_ref, zc_ref, zd_ref, ze_ref, wg_ref, wb_ref, o_ref):
    x = x_ref[...]
    acc = None
    for n, z_ref in enumerate((za_ref, zb_ref, zc_ref, zd_ref, ze_ref)):
        gate = _sigmoid(jnp.dot(x, wg_ref[n], preferred_element_type=F32))
        term = gate * jnp.dot(z_ref[...], wb_ref[n], preferred_element_type=F32)
        acc = term if acc is None else acc + term
    o_ref[...] = acc.astype(BF16)


def gated_merge(xb, zs, wg, wb, tm=TM_MERGE, tn=TN_MERGE):
    t, d = xb.shape
    tm = min(tm, t)
    w = BRANCH_W
    z_spec = pl.BlockSpec((tm, w), lambda i, j: (i, 0))
    return pl.pallas_call(
        _merge_kernel,
        grid=(t // tm, d // tn),
        in_specs=[pl.BlockSpec((tm, d), lambda i, j: (i, 0)),
                  z_spec, z_spec, z_spec, z_spec, z_spec,
                  pl.BlockSpec((N_BRANCH, d, tn), lambda i, j: (0, 0, j)),
                  pl.BlockSpec((N_BRANCH, w, tn), lambda i, j: (0, 0, j))],
        out_specs=pl.BlockSpec((tm, tn), lambda i, j: (i, j)),
        out_shape=jax.ShapeDtypeStruct((t, d), BF16),
        compiler_params=_cparams(("parallel", "parallel")),
        name="gated_merge",
    )(xb, *zs, wg, wb)


def _out_kernel(m_ref, w_ref, x_ref, g_ref, b_ref, of_ref, ob_ref):
    out = jnp.dot(m_ref[...], w_ref[...], preferred_element_type=F32)
    y = _ln_rows(DN_ALPHA * x_ref[...] + out, g_ref[...], b_ref[...])
    of_ref[...] = y
    ob_ref[...] = y.astype(BF16)


def out_proj_ln(merged, w_out, x, g, b, tm=TM_OUT):
    t, d = x.shape
    tm = min(tm, t)
    row = pl.BlockSpec((tm, d), lambda i: (i, 0))
    vec = pl.BlockSpec((1, d), lambda i: (0, 0))
    return pl.pallas_call(
        _out_kernel,
        grid=(t // tm,),
        in_specs=[row, pl.BlockSpec((d, d), lambda i: (0, 0)), row, vec, vec],
        out_specs=[row, row],
        out_shape=[jax.ShapeDtypeStruct((t, d), F32), jax.ShapeDtypeStruct((t, d), BF16)],
        compiler_params=_cparams(("parallel",)),
        name="out_proj_ln",
    )(merged, w_out, x, g.reshape(1, d), b.reshape(1, d))


def _cols(w, *ranges):
    return jnp.concatenate([w[:, a:b] for a, b in ranges], axis=1).astype(BF16)


def hybrid_layer(xf, xb, mem_b, bsz, seq, w_in, w_gate, w_bout, w_out, sinks, sc_w, cf_w, cf_g, cf_b,
                 ml_ib, ml_fb, mem_wkv, ln_g, ln_b):
    mem_len = mem_b.shape[0] // bsz
    w_attn = _cols(w_in, (O_AQ, O_SCH))
    w_conv = _cols(w_in, (O_SCH, O_MLQ))
    w_ml = _cols(w_in, (O_MLQ, O_MLI), (O_MLO, O_XAQ))
    w_xa = _cols(w_in, (O_XAQ, O_SILU))
    w_silu = _cols(w_in, (O_SILU, O_END))
    w_gt = jnp.pad(w_in[:, O_MLI:O_MLO], ((0, 0), (0, LANES - 2 * ML_HEADS))).astype(BF16)
    bias_row = jnp.pad(jnp.concatenate([ml_ib, ml_fb]).astype(F32), (0, LANES - 2 * ML_HEADS)).reshape(1, LANES)

    qkv = matmul(xb, w_attn, BF16, TM_MM, 1280, "proj_attn")
    conv = matmul(xb, w_conv, BF16, TM_MM, 1024, "proj_conv")
    ml = matmul(xb, w_ml, BF16, TM_MM, 1024, "proj_mlstm")
    xq = matmul(xb, w_xa, BF16, TM_MM, 1024, "proj_xa")
    silu = matmul(xb, w_silu, BF16, TM_MM, 1024, "proj_silu")
    gates = matmul(xb, w_gt, F32, TM_MM, LANES, "proj_gates")
    kv = matmul(mem_b, mem_wkv.astype(BF16), BF16, TM_MM, 1024, "mem_kv")

    z_a = window_attention(qkv, silu, sinks, bsz, seq)
    z_b, z_c = conv_mixers(conv, silu, sc_w, cf_w, cf_g, cf_b, bsz, seq)
    z_d = mlstm(ml, gates, bias_row, silu, bsz, seq)
    z_e = cross_attention(xq, kv, silu, bsz, seq, mem_len)

    merged = gated_merge(xb, (z_a, z_b, z_c, z_d, z_e), w_gate.astype(BF16), w_bout.astype(BF16))
    return out_proj_ln(merged, w_out.astype(BF16), xf, ln_g, ln_b)


def kernel(x, mem, ln_in_g, ln_in_b, w_in, w_gate, w_bout, w_out, attn_sinks, sc_w, cf_w, cf_ln_g,
           cf_ln_b, ml_i_bias, ml_f_bias, mem_wkv, ln_g, ln_b):
    bsz, seq, d = x.shape
    xf, xb = layer_norm_in(x.reshape(bsz * seq, d), ln_in_g, ln_in_b)
    mem_b = mem.reshape(-1, d).astype(BF16)
    for l in range(w_in.shape[0]):
        xf, xb = hybrid_layer(xf, xb, mem_b, bsz, seq, w_in[l], w_gate[l], w_bout[l], w_out[l],
                              attn_sinks[l], sc_w[l], cf_w[l], cf_ln_g[l], cf_ln_b[l],
                              ml_i_bias[l], ml_f_bias[l], mem_wkv[l], ln_g[l], ln_b[l])
    return xf.reshape(bsz, seq, d)
```

```python
import functools

import jax
import jax.numpy as jnp
import numpy as np
from jax import lax
from jax.experimental import pallas as pl
from jax.experimental.pallas import tpu as pltpu

F32 = jnp.float32
BF16 = jnp.bfloat16

D_MODEL = 2048
DEPTH = 4
CHUNK = 64
BRANCH_W = 1024
N_BRANCH = 5
A_HEADS = 16
A_KV_HEADS = 2
A_HD = 64
A_WINDOW = 128
SC_WIDTH = 3
CF_WIDTH = 31
ML_HEADS = 4
ML_DK = 128
ML_DV = 256
XA_HEADS = 4
XA_HD = BRANCH_W // XA_HEADS
DN_ALPHA = (2 * DEPTH) ** 0.25
LN_EPS = 1e-5

IN_SPLITS = (
    A_HEADS * A_HD, A_KV_HEADS * A_HD, A_KV_HEADS * A_HD,
    BRANCH_W, BRANCH_W, BRANCH_W,
    BRANCH_W, BRANCH_W,
    ML_HEADS * ML_DK, ML_HEADS * ML_DK, ML_HEADS * ML_DV,
    ML_HEADS, ML_HEADS, BRANCH_W,
    BRANCH_W,
    N_BRANCH * BRANCH_W,
)
_OFF = tuple(int(v) for v in np.cumsum((0,) + IN_SPLITS))
(O_AQ, O_AK, O_AV, O_SCH, O_SCBG, O_SCCG, O_CFV, O_CFG, O_MLQ, O_MLK, O_MLV,
 O_MLI, O_MLF, O_MLO, O_XAQ, O_SILU, O_END) = _OFF

LANES = 128
HALO = 32
ALIBI_SLOPES = tuple(float(2.0 ** (-8.0 * (h + 1) / A_HEADS)) for h in range(A_HEADS))

TM_MM = 1024
TQ_ATTN = 256
TS_CONV = 256
L_MLSTM = 128
TQ_XA = 512
TM_MERGE = 512
TN_MERGE = 256
TM_OUT = 512
VMEM_LIMIT = 56 * 1024 * 1024


def _cparams(sem):
    return pltpu.CompilerParams(dimension_semantics=sem, vmem_limit_bytes=VMEM_LIMIT)


def _sigmoid(x):
    return 1.0 / (1.0 + jnp.exp(-x))


def _silu(x):
    return x * _sigmoid(x)


def _ln_rows(x, g, b):
    mu = jnp.mean(x, axis=-1, keepdims=True)
    xc = x - mu
    var = jnp.mean(xc * xc, axis=-1, keepdims=True)
    return xc * lax.rsqrt(var + LN_EPS) * g + b


def _ln_kernel(x_ref, g_ref, b_ref, of_ref, ob_ref):
    y = _ln_rows(x_ref[...], g_ref[...], b_ref[...])
    of_ref[...] = y
    ob_ref[...] = y.astype(BF16)


def layer_norm_in(x2d, g, b, tm=512):
    t, d = x2d.shape
    return pl.pallas_call(
        _ln_kernel,
        grid=(t // tm,),
        in_specs=[pl.BlockSpec((tm, d), lambda i: (i, 0)),
                  pl.BlockSpec((1, d), lambda i: (0, 0)),
                  pl.BlockSpec((1, d), lambda i: (0, 0))],
        out_specs=[pl.BlockSpec((tm, d), lambda i: (i, 0)),
                   pl.BlockSpec((tm, d), lambda i: (i, 0))],
        out_shape=[jax.ShapeDtypeStruct((t, d), F32), jax.ShapeDtypeStruct((t, d), BF16)],
        compiler_params=_cparams(("parallel",)),
        name="ln_in",
    )(x2d, g.reshape(1, d), b.reshape(1, d))


def _mm_kernel(a_ref, b_ref, o_ref):
    o_ref[...] = jnp.dot(a_ref[...], b_ref[...], preferred_element_type=F32).astype(o_ref.dtype)


def matmul(a, b, layer, out_dtype, tm, tn, name):
    m, k = a.shape
    n = b.shape[2]
    tm = min(tm, m)
    return pl.pallas_call(
        _mm_kernel,
        grid=(m // tm, n // tn),
        in_specs=[pl.BlockSpec((tm, k), lambda i, j: (i, 0)),
                  pl.BlockSpec((None, k, tn), lambda i, j: (layer, 0, j))],
        out_specs=pl.BlockSpec((tm, tn), lambda i, j: (i, j)),
        out_shape=jax.ShapeDtypeStruct((m, n), out_dtype),
        compiler_params=_cparams(("parallel", "parallel")),
        name=name,
    )(a, b)


def _attn_kernel(q_ref, kvc_ref, kvp_ref, path_ref, snk_ref, o_ref, *, tq):
    i = pl.program_id(1)
    nk = A_WINDOW + tq
    kv = jnp.concatenate([kvp_ref[...], kvc_ref[...]], axis=0)
    t_pos = lax.broadcasted_iota(jnp.int32, (tq, nk), 0)
    s_pos = lax.broadcasted_iota(jnp.int32, (tq, nk), 1) - A_WINDOW
    q_chunk = t_pos // CHUNK
    k_chunk = (s_pos + A_WINDOW) // CHUNK - A_WINDOW // CHUNK
    in_band = jnp.logical_and(k_chunk <= q_chunk, k_chunk >= q_chunk - A_WINDOW // CHUNK)
    exists = jnp.logical_or(s_pos >= 0, i > 0)
    dist = jnp.abs(t_pos - s_pos).astype(F32)
    neg_dist = jnp.where(jnp.logical_and(in_band, exists), -dist, -jnp.inf)
    grp = A_HEADS // A_KV_HEADS
    scale = A_HD ** -0.5
    for hp in range(A_HEADS // 2):
        pair = []
        for h in (2 * hp, 2 * hp + 1):
            g = h // grp
            qh = q_ref[:, h * A_HD:(h + 1) * A_HD]
            kg = kv[:, g * A_HD:(g + 1) * A_HD]
            vg = kv[:, A_KV_HEADS * A_HD + g * A_HD:A_KV_HEADS * A_HD + (g + 1) * A_HD]
            s = lax.dot_general(qh, kg, (((1,), (1,)), ((), ())), preferred_element_type=F32)
            s = s * scale + ALIBI_SLOPES[h] * neg_dist
            snk = snk_ref[h:h + 1, 0:1]
            mx = jnp.maximum(jnp.max(s, axis=-1, keepdims=True), snk)
            p = jnp.exp(s - mx)
            den = jnp.sum(p, axis=-1, keepdims=True) + jnp.exp(snk - mx)
            o = jnp.dot(p.astype(BF16), vg, preferred_element_type=F32)
            pair.append(o / den)
        y = jnp.concatenate(pair, axis=1)
        path = path_ref[:, hp * LANES:(hp + 1) * LANES].astype(F32)
        o_ref[:, hp * LANES:(hp + 1) * LANES] = (y * _silu(path)).astype(BF16)


def window_attention(qkv, silu, path_col, sinks, bsz, seq, tq=TQ_ATTN):
    t = bsz * seq
    nq = seq // tq
    kvw = 2 * A_KV_HEADS * A_HD
    kv_col = (A_HEADS * A_HD) // kvw
    snk = jnp.broadcast_to(sinks.astype(F32)[:, None], (A_HEADS, LANES))
    return pl.pallas_call(
        functools.partial(_attn_kernel, tq=tq),
        grid=(bsz, nq),
        in_specs=[
            pl.BlockSpec((tq, A_HEADS * A_HD), lambda b, i: (b * nq + i, 0)),
            pl.BlockSpec((tq, kvw), lambda b, i: (b * nq + i, kv_col)),
            pl.BlockSpec((A_WINDOW, kvw),
                         lambda b, i: (jnp.maximum((b * nq + i) * (tq // A_WINDOW) - 1, 0), kv_col)),
            pl.BlockSpec((tq, BRANCH_W), lambda b, i: (b * nq + i, path_col)),
            pl.BlockSpec((A_HEADS, LANES), lambda b, i: (0, 0)),
        ],
        out_specs=pl.BlockSpec((tq, BRANCH_W), lambda b, i: (b * nq + i, 0)),
        out_shape=jax.ShapeDtypeStruct((t, BRANCH_W), BF16),
        compiler_params=_cparams(("parallel", "parallel")),
        name="window_attn",
    )(qkv, qkv, qkv, silu, snk)


CV_SLAB = 256
CV_GROUPS = 7


def _conv_proj_kernel(x_ref, w_ref, scw_ref, cfw_ref, lng_ref, lnb_ref, zb_ref, zc_ref,
                      ub_scr, uc_scr, cv_scr, pc_scr, *, ts):
    i = pl.program_id(1)
    n_sub = BRANCH_W // LANES

    @pl.when(i == 0)
    def _():
        ub_scr[:, 0:HALO, :] = jnp.zeros((n_sub, HALO, LANES), F32)
        uc_scr[:, 0:HALO, :] = jnp.zeros((n_sub, HALO, LANES), F32)

    x = x_ref[...]
    for s4 in range(BRANCH_W // CV_SLAB):
        c0 = s4 * CV_GROUPS * CV_SLAB
        proj = jnp.dot(x, w_ref[:, c0:c0 + CV_GROUPS * CV_SLAB], preferred_element_type=F32)
        for half in range(CV_SLAB // LANES):
            s = s4 * (CV_SLAB // LANES) + half
            ln = slice(s * LANES, (s + 1) * LANES)

            def grp(g, half=half, proj=proj):
                return proj[:, g * CV_SLAB + half * LANES:g * CV_SLAB + (half + 1) * LANES]

            ub_scr[s, HALO:HALO + ts, :] = grp(2) * grp(0)
            uc_scr[s, HALO:HALO + ts, :] = grp(3) * _sigmoid(grp(4))
            acc = jnp.zeros((ts, LANES), F32)
            for j in range(SC_WIDTH):
                win = ub_scr[s, pl.ds(HALO - (SC_WIDTH - 1) + j, ts, stride=1), :]
                acc = acc + scw_ref[j:j + 1, ln] * win
            zb_ref[:, ln] = (grp(1) * acc * _silu(grp(5))).astype(BF16)
            acc = jnp.zeros((ts, LANES), F32)
            for j in range(CF_WIDTH):
                win = uc_scr[s, pl.ds(HALO - (CF_WIDTH - 1) + j, ts, stride=1), :]
                acc = acc + cfw_ref[j:j + 1, ln] * win
            cv_scr[:, ln] = acc
            pc_scr[:, ln] = _silu(grp(6))
            ub_scr[s, 0:HALO, :] = ub_scr[s, ts:ts + HALO, :]
            uc_scr[s, 0:HALO, :] = uc_scr[s, ts:ts + HALO, :]

    yc = _silu(_ln_rows(cv_scr[...], lng_ref[...], lnb_ref[...]))
    zc_ref[...] = (yc * pc_scr[...]).astype(BF16)


def conv_mixers_proj(xb, w_cv, layer, sc_w, cf_w, cf_g, cf_b, bsz, seq, ts=TS_CONV):
    t, d = xb.shape
    ns = seq // ts
    w = BRANCH_W
    n_sub = w // LANES

    def full(shape):
        return pl.BlockSpec(shape, lambda b, i: (0, 0))

    out_spec = pl.BlockSpec((ts, w), lambda b, i: (b * ns + i, 0))
    return pl.pallas_call(
        functools.partial(_conv_proj_kernel, ts=ts),
        grid=(bsz, ns),
        in_specs=[pl.BlockSpec((ts, d), lambda b, i: (b * ns + i, 0)),
                  pl.BlockSpec((None,) + w_cv.shape[1:], lambda b, i: (layer, 0, 0),
                               pipeline_mode=pl.Buffered(1)),
                  full((SC_WIDTH, w)), full((CF_WIDTH, w)), full((1, w)), full((1, w))],
        out_specs=[out_spec, out_spec],
        out_shape=[jax.ShapeDtypeStruct((t, w), BF16), jax.ShapeDtypeStruct((t, w), BF16)],
        scratch_shapes=[pltpu.VMEM((n_sub, HALO + ts, LANES), F32),
                        pltpu.VMEM((n_sub, HALO + ts, LANES), F32),
                        pltpu.VMEM((ts, w), F32), pltpu.VMEM((ts, w), F32)],
        compiler_params=_cparams(("parallel", "arbitrary")),
        name="conv_mixers_proj",
    )(xb, w_cv, sc_w.astype(F32), cf_w.astype(F32), cf_g.reshape(1, w).astype(F32), cf_b.reshape(1, w).astype(F32))


def conv_proj_weight(w_in):
    offs = (O_SCH, O_SCBG, O_SCCG, O_CFV, O_CFG, O_SILU + BRANCH_W, O_SILU + 2 * BRANCH_W)
    cols = [w_in[..., o + s * CV_SLAB:o + (s + 1) * CV_SLAB]
            for s in range(BRANCH_W // CV_SLAB) for o in offs]
    return jnp.concatenate(cols, axis=-1).astype(BF16)


def _log_sigmoid(x):
    return jnp.minimum(x, 0.0) - jnp.log(1.0 + jnp.exp(-jnp.abs(x)))


def _mlstm_kernel(q_ref, k_ref, v_ref, o_ref, g_ref, bias_ref, path_ref, z_ref, c_scr, m_scr, *, lc):
    c = pl.program_id(1)

    @pl.when(c == 0)
    def _():
        c_scr[...] = jnp.zeros_like(c_scr)
        m_scr[...] = jnp.zeros_like(m_scr)

    gates = g_ref[...] + bias_ref[...]
    logf = _log_sigmoid(gates)
    row = lax.broadcasted_iota(jnp.int32, (lc, lc), 0)
    col = lax.broadcasted_iota(jnp.int32, (lc, lc), 1)
    causal = col <= row
    bcum = jnp.dot(causal.astype(F32), logf, preferred_element_type=F32,
                   precision=lax.Precision.HIGHEST)
    gates_t = gates.T
    bcum_t = bcum.T
    ones = jnp.ones((lc, LANES), BF16)
    scale = ML_DK ** -0.5
    for h in range(ML_HEADS):
        fl = ML_HEADS + h
        b_col = bcum[:, fl:fl + 1]
        b_row = bcum_t[fl:fl + 1, :]
        i_col = gates[:, h:h + 1]
        i_row = gates_t[h:h + 1, :]
        m_prev = m_scr[h:h + 1, 0:1]
        dmat = jnp.where(causal, b_col - b_row + i_row, -jnp.inf)
        inter = b_col + m_prev
        m_t = jnp.maximum(inter, jnp.max(dmat, axis=-1, keepdims=True))
        w_intra = jnp.exp(dmat - m_t)
        w_inter = jnp.exp(inter - m_t)
        qh = q_ref[:, h * ML_DK:(h + 1) * ML_DK]
        kh = k_ref[:, h * ML_DK:(h + 1) * ML_DK]
        v_ext = jnp.concatenate([v_ref[:, h * ML_DV:(h + 1) * ML_DV], ones], axis=1)
        qk = lax.dot_general(qh, kh, (((1,), (1,)), ((), ())), preferred_element_type=F32)
        qk = qk * scale * w_intra
        state = c_scr[h]
        num_ext = (jnp.dot(qk.astype(BF16), v_ext, preferred_element_type=F32)
                   + w_inter * jnp.dot(qh, state.astype(BF16), preferred_element_type=F32))
        num = num_ext[:, :ML_DV]
        den = num_ext[:, ML_DV:]
        den = jnp.maximum(jnp.abs(den), jnp.exp(-m_t))
        hh = num / jnp.concatenate([den, den], axis=1)
        b_last = bcum[lc - 1:lc, fl:fl + 1]
        m_new = m_t[lc - 1:lc, :]
        w_state = jnp.exp(b_last - b_col + i_col - m_new)
        decay = jnp.exp(b_last + m_prev - m_new)
        kw_t = (kh.astype(F32) * (scale * w_state)).T.astype(BF16)
        c_scr[h] = decay * state + jnp.dot(kw_t, v_ext, preferred_element_type=F32)
        m_scr[h:h + 1, :] = jnp.broadcast_to(m_new, (1, LANES))
        og = o_ref[:, h * ML_DV:(h + 1) * ML_DV].astype(F32)
        path = path_ref[:, h * ML_DV:(h + 1) * ML_DV].astype(F32)
        z_ref[:, h * ML_DV:(h + 1) * ML_DV] = (_sigmoid(og) * hh * _silu(path)).astype(BF16)


def mlstm(ml, gates, bias_row, silu, path_col, bsz, seq, lc=L_MLSTM):
    t = bsz * seq
    nc = seq // lc
    qw = ML_HEADS * ML_DK
    vw = ML_HEADS * ML_DV
    return pl.pallas_call(
        functools.partial(_mlstm_kernel, lc=lc),
        grid=(bsz, nc),
        in_specs=[
            pl.BlockSpec((lc, qw), lambda b, c: (b * nc + c, 0)),
            pl.BlockSpec((lc, qw), lambda b, c: (b * nc + c, 1)),
            pl.BlockSpec((lc, vw), lambda b, c: (b * nc + c, 1)),
            pl.BlockSpec((lc, vw), lambda b, c: (b * nc + c, 2)),
            pl.BlockSpec((lc, LANES), lambda b, c: (b * nc + c, 0)),
            pl.BlockSpec((1, LANES), lambda b, c: (0, 0)),
            pl.BlockSpec((lc, BRANCH_W), lambda b, c: (b * nc + c, path_col)),
        ],
        out_specs=pl.BlockSpec((lc, BRANCH_W), lambda b, c: (b * nc + c, 0)),
        out_shape=jax.ShapeDtypeStruct((t, BRANCH_W), BF16),
        scratch_shapes=[pltpu.VMEM((ML_HEADS, ML_DK, ML_DV + LANES), F32),
                        pltpu.VMEM((8, LANES), F32)],
        compiler_params=_cparams(("parallel", "arbitrary")),
        name="mlstm",
    )(ml, ml, ml, ml, gates, bias_row, silu)


def _xattn_kernel(q_ref, kv_ref, path_ref, o_ref):
    scale = XA_HD ** -0.5
    for h in range(XA_HEADS):
        sl = slice(h * XA_HD, (h + 1) * XA_HD)
        kh = kv_ref[:, sl]
        vh = kv_ref[:, BRANCH_W + h * XA_HD:BRANCH_W + (h + 1) * XA_HD]
        s = lax.dot_general(q_ref[:, sl], kh, (((1,), (1,)), ((), ())),
                            preferred_element_type=F32) * scale
        mx = jnp.max(s, axis=-1, keepdims=True)
        p = jnp.exp(s - mx)
        den = jnp.sum(p, axis=-1, keepdims=True)
        o = jnp.dot(p.astype(BF16), vh, preferred_element_type=F32) / den
        o_ref[:, sl] = (o * _silu(path_ref[:, sl].astype(F32))).astype(BF16)


def cross_attention(xq, kv, silu, path_col, bsz, seq, mem_len, tq=TQ_XA):
    t = bsz * seq
    nq = seq // tq
    return pl.pallas_call(
        _xattn_kernel,
        grid=(bsz, nq),
        in_specs=[
            pl.BlockSpec((tq, BRANCH_W), lambda b, i: (b * nq + i, 0)),
            pl.BlockSpec((mem_len, 2 * BRANCH_W), lambda b, i: (b, 0)),
            pl.BlockSpec((tq, BRANCH_W), lambda b, i: (b * nq + i, path_col)),
        ],
        out_specs=pl.BlockSpec((tq, BRANCH_W), lambda b, i: (b * nq + i, 0)),
        out_shape=jax.ShapeDtypeStruct((t, BRANCH_W), BF16),
        compiler_params=_cparams(("parallel", "parallel")),
        name="cross_attn",
    )(xq, kv, silu)


def _merge_kernel(x_ref, za_ref, zb_ref, zc_ref, zd_ref, ze_ref, wg_ref, wb_ref, o_ref):
    x = x_ref[...]
    acc = None
    for n, z_ref in enumerate((za_ref, zb_ref, zc_ref, zd_ref, ze_ref)):
        gate = _sigmoid(jnp.dot(x, wg_ref[n], preferred_element_type=F32))
        term = gate * jnp.dot(z_ref[...], wb_ref[n], preferred_element_type=F32)
        acc = term if acc is None else acc + term
    o_ref[...] = acc.astype(BF16)


def gated_merge(xb, zs, wg, wb, layer, tm=TM_MERGE, tn=TN_MERGE):
    t, d = xb.shape
    tm = min(tm, t)
    w = BRANCH_W
    z_spec = pl.BlockSpec((tm, w), lambda i, j: (i, 0))
    return pl.pallas_call(
        _merge_kernel,
        grid=(t // tm, d // tn),
        in_specs=[pl.BlockSpec((tm, d), lambda i, j: (i, 0)),
                  z_spec, z_spec, z_spec, z_spec, z_spec,
                  pl.BlockSpec((None, N_BRANCH, d, tn), lambda i, j: (layer, 0, 0, j)),
                  pl.BlockSpec((None, N_BRANCH, w, tn), lambda i, j: (layer, 0, 0, j))],
        out_specs=pl.BlockSpec((tm, tn), lambda i, j: (i, j)),
        out_shape=jax.ShapeDtypeStruct((t, d), BF16),
        compiler_params=_cparams(("parallel", "parallel")),
        name="gated_merge",
    )(xb, *zs, wg, wb)


def _out_kernel(m_ref, w_ref, x_ref, g_ref, b_ref, of_ref, ob_ref):
    out = jnp.dot(m_ref[...], w_ref[...], preferred_element_type=F32)
    y = _ln_rows(DN_ALPHA * x_ref[...] + out, g_ref[...], b_ref[...])
    of_ref[...] = y
    ob_ref[...] = y.astype(BF16)


def out_proj_ln(merged, w_out, layer, x, g, b, tm=TM_OUT):
    t, d = x.shape
    tm = min(tm, t)
    row = pl.BlockSpec((tm, d), lambda i: (i, 0))
    vec = pl.BlockSpec((1, d), lambda i: (0, 0))
    return pl.pallas_call(
        _out_kernel,
        grid=(t // tm,),
        in_specs=[row,
                  pl.BlockSpec((None, d, d), lambda i: (layer, 0, 0), pipeline_mode=pl.Buffered(1)),
                  row, vec, vec],
        out_specs=[row, row],
        out_shape=[jax.ShapeDtypeStruct((t, d), F32), jax.ShapeDtypeStruct((t, d), BF16)],
        compiler_params=_cparams(("parallel",)),
        name="out_proj_ln",
    )(merged, w_out, x, g.reshape(1, d), b.reshape(1, d))


def _cols(w, *ranges):
    return jnp.concatenate([w[..., a:b] for a, b in ranges], axis=-1).astype(BF16)


def prepare_weights(w_in, w_gate, w_bout, w_out, mem_wkv, ml_i_bias, ml_f_bias):
    pad = LANES - 2 * ML_HEADS
    return dict(
        attn=_cols(w_in, (O_AQ, O_SCH)),
        conv=conv_proj_weight(w_in),
        ml=_cols(w_in, (O_MLQ, O_MLI), (O_MLO, O_XAQ)),
        xa=_cols(w_in, (O_XAQ, O_SILU)),
        silu=_cols(w_in, (O_SILU, O_SILU + BRANCH_W), (O_SILU + 3 * BRANCH_W, O_END)),
        gates=jnp.pad(w_in[..., O_MLI:O_MLO], ((0, 0), (0, 0), (0, pad))).astype(BF16),
        bias=jnp.pad(jnp.concatenate([ml_i_bias, ml_f_bias], axis=-1).astype(F32), ((0, 0), (0, pad))),
        kv=mem_wkv.astype(BF16),
        wg=w_gate.astype(BF16),
        wb=w_bout.astype(BF16),
        wo=w_out.astype(BF16),
    )


def hybrid_layer(xf, xb, mem_b, bsz, seq, wts, l, sinks, sc_w, cf_w, cf_g, cf_b, ln_g, ln_b):
    mem_len = mem_b.shape[0] // bsz
    qkv = matmul(xb, wts["attn"], l, BF16, TM_MM, 1280, "proj_attn")
    ml = matmul(xb, wts["ml"], l, BF16, TM_MM, 1024, "proj_mlstm")
    xq = matmul(xb, wts["xa"], l, BF16, TM_MM, 1024, "proj_xa")
    silu = matmul(xb, wts["silu"], l, BF16, TM_MM, 1024, "proj_silu")
    gates = matmul(xb, wts["gates"], l, F32, TM_MM, LANES, "proj_gates")
    kv = matmul(mem_b, wts["kv"], l, BF16, TM_MM, 1024, "mem_kv")

    z_a = window_attention(qkv, silu, 0, sinks, bsz, seq)
    z_b, z_c = conv_mixers_proj(xb, wts["conv"], l, sc_w, cf_w, cf_g, cf_b, bsz, seq)
    z_d = mlstm(ml, gates, wts["bias"][l].reshape(1, LANES), silu, 1, bsz, seq)
    z_e = cross_attention(xq, kv, silu, 2, bsz, seq, mem_len)

    merged = gated_merge(xb, (z_a, z_b, z_c, z_d, z_e), wts["wg"], wts["wb"], l)
    return out_proj_ln(merged, wts["wo"], l, xf, ln_g, ln_b)


def kernel(x, mem, ln_in_g, ln_in_b, w_in, w_gate, w_bout, w_out, attn_sinks, sc_w, cf_w, cf_ln_g,
           cf_ln_b, ml_i_bias, ml_f_bias, mem_wkv, ln_g, ln_b):
    bsz, seq, d = x.shape
    wts = prepare_weights(w_in, w_gate, w_bout, w_out, mem_wkv, ml_i_bias, ml_f_bias)
    xf, xb = layer_norm_in(x.reshape(bsz * seq, d), ln_in_g, ln_in_b)
    mem_b = mem.reshape(-1, d).astype(BF16)
    for l in range(w_in.shape[0]):
        xf, xb = hybrid_layer(xf, xb, mem_b, bsz, seq, wts, l, attn_sinks[l], sc_w[l], cf_w[l],
                              cf_ln_g[l], cf_ln_b[l], ln_g[l], ln_b[l])
    return xf.reshape(bsz, seq, d)
```

```python
import functools

import jax
import jax.numpy as jnp
import numpy as np
from jax import lax
from jax.experimental import pallas as pl
from jax.experimental.pallas import tpu as pltpu

F32 = jnp.float32
BF16 = jnp.bfloat16

D_MODEL = 2048
DEPTH = 4
CHUNK = 64
BRANCH_W = 1024
N_BRANCH = 5
A_HEADS = 16
A_KV_HEADS = 2
A_HD = 64
A_WINDOW = 128
SC_WIDTH = 3
CF_WIDTH = 31
ML_HEADS = 4
ML_DK = 128
ML_DV = 256
XA_HEADS = 4
XA_HD = BRANCH_W // XA_HEADS
DN_ALPHA = (2 * DEPTH) ** 0.25
LN_EPS = 1e-5

IN_SPLITS = (
    A_HEADS * A_HD, A_KV_HEADS * A_HD, A_KV_HEADS * A_HD,
    BRANCH_W, BRANCH_W, BRANCH_W,
    BRANCH_W, BRANCH_W,
    ML_HEADS * ML_DK, ML_HEADS * ML_DK, ML_HEADS * ML_DV,
    ML_HEADS, ML_HEADS, BRANCH_W,
    BRANCH_W,
    N_BRANCH * BRANCH_W,
)
_OFF = tuple(int(v) for v in np.cumsum((0,) + IN_SPLITS))
(O_AQ, O_AK, O_AV, O_SCH, O_SCBG, O_SCCG, O_CFV, O_CFG, O_MLQ, O_MLK, O_MLV,
 O_MLI, O_MLF, O_MLO, O_XAQ, O_SILU, O_END) = _OFF

LANES = 128
HALO = 32
ALIBI_SLOPES = tuple(float(2.0 ** (-8.0 * (h + 1) / A_HEADS)) for h in range(A_HEADS))

TM_MM = 1024
TQ_ATTN = 256
TS_CONV = 256
L_MLSTM = 128
TT_MLSTM = 256
TQ_XA = 512
TM_MERGE = 1024
TN_MERGE = 256
TM_OUT = 512
VMEM_LIMIT = 56 * 1024 * 1024


def _cparams(sem):
    return pltpu.CompilerParams(dimension_semantics=sem, vmem_limit_bytes=VMEM_LIMIT)


def _sigmoid(x):
    return 1.0 / (1.0 + jnp.exp(-x))


def _silu(x):
    return x * _sigmoid(x)


def _ln_rows(x, g, b):
    mu = jnp.mean(x, axis=-1, keepdims=True)
    xc = x - mu
    var = jnp.mean(xc * xc, axis=-1, keepdims=True)
    return xc * lax.rsqrt(var + LN_EPS) * g + b


def _ln_kernel(x_ref, g_ref, b_ref, of_ref, ob_ref):
    y = _ln_rows(x_ref[...], g_ref[...], b_ref[...])
    of_ref[...] = y
    ob_ref[...] = y.astype(BF16)


def layer_norm_in(x2d, g, b, tm=512):
    t, d = x2d.shape
    return pl.pallas_call(
        _ln_kernel,
        grid=(t // tm,),
        in_specs=[pl.BlockSpec((tm, d), lambda i: (i, 0)),
                  pl.BlockSpec((1, d), lambda i: (0, 0)),
                  pl.BlockSpec((1, d), lambda i: (0, 0))],
        out_specs=[pl.BlockSpec((tm, d), lambda i: (i, 0)),
                   pl.BlockSpec((tm, d), lambda i: (i, 0))],
        out_shape=[jax.ShapeDtypeStruct((t, d), F32), jax.ShapeDtypeStruct((t, d), BF16)],
        compiler_params=_cparams(("parallel",)),
        name="ln_in",
    )(x2d, g.reshape(1, d), b.reshape(1, d))


def _mm_kernel(a_ref, b_ref, o_ref):
    o_ref[...] = jnp.dot(a_ref[...], b_ref[...], preferred_element_type=F32).astype(o_ref.dtype)


def matmul(a, b, layer, out_dtype, tm, tn, name):
    m, k = a.shape
    n = b.shape[2]
    tm = min(tm, m)
    return pl.pallas_call(
        _mm_kernel,
        grid=(m // tm, n // tn),
        in_specs=[pl.BlockSpec((tm, k), lambda i, j: (i, 0)),
                  pl.BlockSpec((None, k, tn), lambda i, j: (layer, 0, j))],
        out_specs=pl.BlockSpec((tm, tn), lambda i, j: (i, j)),
        out_shape=jax.ShapeDtypeStruct((m, n), out_dtype),
        compiler_params=_cparams(("parallel", "parallel")),
        name=name,
    )(a, b)


QB_ATTN = A_WINDOW


def _attn_kernel(q_ref, kvc_ref, kvp_ref, path_ref, snk_ref, bias_ref, o_ref, *, tq):
    i = pl.program_id(1)
    nk = A_WINDOW + tq
    kv = jnp.concatenate([kvp_ref[...], kvc_ref[...]], axis=0)
    low = lax.broadcasted_iota(jnp.int32, (nk, LANES), 1) < A_HD
    low_q = lax.broadcasted_iota(jnp.int32, (QB_ATTN, LANES), 1) < A_HD
    zero = jnp.zeros((nk, LANES), BF16)

    def halves(blk):
        swapped = jnp.concatenate([blk[:, A_HD:], blk[:, :A_HD]], axis=1)
        return {(0, 0): jnp.where(low, blk, zero), (0, 1): jnp.where(low, zero, swapped),
                (1, 0): jnp.where(low, swapped, zero), (1, 1): jnp.where(low, zero, blk)}

    k_half = halves(kv[:, :LANES] * (A_HD ** -0.5))
    v_half = halves(kv[:, LANES:])
    for r in range(tq // QB_ATTN):
        rows = slice(r * QB_ATTN, (r + 1) * QB_ATTN)
        keys = slice(r * QB_ATTN, r * QB_ATTN + 2 * A_WINDOW)
        tbl = jnp.where(i == 0, 0, 1) if r == 0 else 1
        for g in range(A_KV_HEADS):
            pairs = range(g * ATTN_PAIRS, (g + 1) * ATTN_PAIRS)
            q_stack = jnp.concatenate([q_ref[rows, hp * LANES:(hp + 1) * LANES] for hp in pairs], axis=0)
            o_stack = None
            rden = []
            for par in range(2):
                s = lax.dot_general(q_stack, k_half[g, par][keys], (((1,), (1,)), ((), ())),
                                    preferred_element_type=F32)
                s = s + bias_ref[tbl, g, par]
                snk = jnp.concatenate(
                    [jnp.broadcast_to(snk_ref[2 * hp + par:2 * hp + par + 1, 0:1], (QB_ATTN, 1)) for hp in pairs],
                    axis=0)
                mx = jnp.maximum(jnp.max(s, axis=-1, keepdims=True), snk)
                p = jnp.exp(s - mx)
                den = jnp.sum(p, axis=-1, keepdims=True) + jnp.exp(snk - mx)
                o = jnp.dot(p.astype(BF16), v_half[g, par][keys], preferred_element_type=F32)
                o_stack = o if o_stack is None else o_stack + o
                rden.append(1.0 / den)
            for j, hp in enumerate(pairs):
                blk = slice(j * QB_ATTN, (j + 1) * QB_ATTN)
                y = o_stack[blk] * jnp.where(low_q, rden[0][blk], rden[1][blk])
                path = path_ref[rows, hp * LANES:(hp + 1) * LANES].astype(F32)
                o_ref[rows, hp * LANES:(hp + 1) * LANES] = (y * _silu(path)).astype(BF16)


ATTN_PAIRS = A_HEADS // A_KV_HEADS // 2


def attn_bias_tables():
    t = np.arange(QB_ATTN)[:, None]
    s = np.arange(2 * A_WINDOW)[None, :]
    q_chunk = t // CHUNK + A_WINDOW // CHUNK
    k_chunk = s // CHUNK
    valid = (k_chunk <= q_chunk) & (k_chunk >= q_chunk - A_WINDOW // CHUNK)
    dist = np.abs(t + A_WINDOW - s).astype(np.float32)
    slopes = np.asarray(ALIBI_SLOPES, np.float32)[:, None, None]
    general = np.where(valid[None], -slopes * dist[None], -np.inf)
    first = np.where((valid & (s >= A_WINDOW))[None], -slopes * dist[None], -np.inf)
    per_head = np.stack([first, general]).astype(np.float32)
    tab = per_head.reshape(2, A_KV_HEADS, ATTN_PAIRS, 2, QB_ATTN, 2 * A_WINDOW).transpose(0, 1, 3, 2, 4, 5)
    return jnp.asarray(tab.reshape(2, A_KV_HEADS, 2, ATTN_PAIRS * QB_ATTN, 2 * A_WINDOW))


def window_attention(qpkv, sinks, bsz, seq, tq=TQ_ATTN):
    t = bsz * seq
    nq = seq // tq
    kvw = 2 * A_KV_HEADS * A_HD
    kv_col = (A_HEADS * A_HD + BRANCH_W) // kvw
    snk = jnp.broadcast_to(sinks.astype(F32)[:, None], (A_HEADS, LANES))
    bias = attn_bias_tables()
    return pl.pallas_call(
        functools.partial(_attn_kernel, tq=tq),
        grid=(bsz, nq),
        in_specs=[
            pl.BlockSpec((tq, A_HEADS * A_HD), lambda b, i: (b * nq + i, 0)),
            pl.BlockSpec((tq, kvw), lambda b, i: (b * nq + i, kv_col)),
            pl.BlockSpec((A_WINDOW, kvw),
                         lambda b, i: (jnp.maximum((b * nq + i) * (tq // A_WINDOW) - 1, 0), kv_col)),
            pl.BlockSpec((tq, BRANCH_W), lambda b, i: (b * nq + i, 1)),
            pl.BlockSpec((A_HEADS, LANES), lambda b, i: (0, 0)),
            pl.BlockSpec(bias.shape, lambda b, i: (0,) * bias.ndim),
        ],
        out_specs=pl.BlockSpec((tq, BRANCH_W), lambda b, i: (b * nq + i, 0)),
        out_shape=jax.ShapeDtypeStruct((t, BRANCH_W), BF16),
        compiler_params=_cparams(("parallel", "parallel")),
        name="window_attn",
    )(qpkv, qpkv, qpkv, qpkv, snk, bias)


CV_SLAB = 256
CV_GROUPS = 7


def _conv_proj_kernel(x_ref, w_ref, scw_ref, cfw_ref, lng_ref, lnb_ref, zb_ref, zc_ref,
                      ub_scr, uc_scr, cv_scr, pc_scr, *, ts):
    i = pl.program_id(1)
    n_sub = BRANCH_W // LANES

    @pl.when(i == 0)
    def _():
        ub_scr[:, 0:HALO, :] = jnp.zeros((n_sub, HALO, LANES), F32)
        uc_scr[:, 0:HALO, :] = jnp.zeros((n_sub, HALO, LANES), F32)

    x = x_ref[...]
    for s4 in range(BRANCH_W // CV_SLAB):
        c0 = s4 * CV_GROUPS * CV_SLAB
        proj = jnp.dot(x, w_ref[:, c0:c0 + CV_GROUPS * CV_SLAB], preferred_element_type=F32)
        for half in range(CV_SLAB // LANES):
            s = s4 * (CV_SLAB // LANES) + half
            ln = slice(s * LANES, (s + 1) * LANES)

            def grp(g, half=half, proj=proj):
                return proj[:, g * CV_SLAB + half * LANES:g * CV_SLAB + (half + 1) * LANES]

            ub_scr[s, HALO:HALO + ts, :] = grp(2) * grp(0)
            uc_scr[s, HALO:HALO + ts, :] = grp(3) * _sigmoid(grp(4))
            acc = jnp.zeros((ts, LANES), F32)
            for j in range(SC_WIDTH):
                win = ub_scr[s, pl.ds(HALO - (SC_WIDTH - 1) + j, ts, stride=1), :]
                acc = acc + scw_ref[j:j + 1, ln] * win
            zb_ref[:, ln] = (grp(1) * acc * _silu(grp(5))).astype(BF16)
            acc = jnp.zeros((ts, LANES), F32)
            for j in range(CF_WIDTH):
                win = uc_scr[s, pl.ds(HALO - (CF_WIDTH - 1) + j, ts, stride=1), :]
                acc = acc + cfw_ref[j:j + 1, ln] * win
            cv_scr[:, ln] = acc
            pc_scr[:, ln] = _silu(grp(6))
            ub_scr[s, 0:HALO, :] = ub_scr[s, ts:ts + HALO, :]
            uc_scr[s, 0:HALO, :] = uc_scr[s, ts:ts + HALO, :]

    yc = _silu(_ln_rows(cv_scr[...], lng_ref[...], lnb_ref[...]))
    zc_ref[...] = (yc * pc_scr[...]).astype(BF16)


def conv_mixers_proj(xb, w_cv, layer, sc_w, cf_w, cf_g, cf_b, bsz, seq, ts=TS_CONV):
    t, d = xb.shape
    ns = seq // ts
    w = BRANCH_W
    n_sub = w // LANES

    def full(shape):
        return pl.BlockSpec(shape, lambda b, i: (0, 0))

    out_spec = pl.BlockSpec((ts, w), lambda b, i: (b * ns + i, 0))
    return pl.pallas_call(
        functools.partial(_conv_proj_kernel, ts=ts),
        grid=(bsz, ns),
        in_specs=[pl.BlockSpec((ts, d), lambda b, i: (b * ns + i, 0)),
                  pl.BlockSpec((None,) + w_cv.shape[1:], lambda b, i: (layer, 0, 0),
                               pipeline_mode=pl.Buffered(1)),
                  full((SC_WIDTH, w)), full((CF_WIDTH, w)), full((1, w)), full((1, w))],
        out_specs=[out_spec, out_spec],
        out_shape=[jax.ShapeDtypeStruct((t, w), BF16), jax.ShapeDtypeStruct((t, w), BF16)],
        scratch_shapes=[pltpu.VMEM((n_sub, HALO + ts, LANES), F32),
                        pltpu.VMEM((n_sub, HALO + ts, LANES), F32),
                        pltpu.VMEM((ts, w), F32), pltpu.VMEM((ts, w), F32)],
        compiler_params=_cparams(("parallel", "arbitrary")),
        name="conv_mixers_proj",
    )(xb, w_cv, sc_w.astype(F32), cf_w.astype(F32), cf_g.reshape(1, w).astype(F32), cf_b.reshape(1, w).astype(F32))


def conv_proj_weight(w16):
    offs = (O_SCH, O_SCBG, O_SCCG, O_CFV, O_CFG, O_SILU + BRANCH_W, O_SILU + 2 * BRANCH_W)
    cols = [w16[..., o + s * CV_SLAB:o + (s + 1) * CV_SLAB]
            for s in range(BRANCH_W // CV_SLAB) for o in offs]
    return jnp.concatenate(cols, axis=-1)


def _log_sigmoid(x):
    return jnp.minimum(x, 0.0) - jnp.log(1.0 + jnp.exp(-jnp.abs(x)))


ML_HEAD_COLS = 2 * ML_DK + 3 * ML_DV


def _mlstm_proj_kernel(x_ref, w_ref, bias_ref, z_ref, c_scr, m_scr, *, tt, lc):
    c = pl.program_id(1)

    @pl.when(c == 0)
    def _():
        c_scr[...] = jnp.zeros_like(c_scr)
        m_scr[...] = jnp.zeros_like(m_scr)

    x = x_ref[...]
    gates_all = jnp.dot(x, w_ref[:, 0:LANES], preferred_element_type=F32) + bias_ref[...]
    row = lax.broadcasted_iota(jnp.int32, (lc, lc), 0)
    col = lax.broadcasted_iota(jnp.int32, (lc, lc), 1)
    causal = col <= row
    tri = causal.astype(F32)
    n_chunks = tt // lc
    per_chunk = []
    for ci in range(n_chunks):
        gates = gates_all[ci * lc:(ci + 1) * lc]
        bcum = jnp.dot(tri, _log_sigmoid(gates), preferred_element_type=F32,
                       precision=lax.Precision.HIGHEST)
        per_chunk.append((gates, gates.T, bcum, bcum.T))
    ones = jnp.ones((lc, LANES), BF16)
    scale = ML_DK ** -0.5
    for h in range(ML_HEADS):
        c0 = LANES + h * ML_HEAD_COLS
        proj = jnp.dot(x, w_ref[:, c0:c0 + ML_HEAD_COLS], preferred_element_type=F32)
        fl = ML_HEADS + h
        for ci in range(n_chunks):
            rows = slice(ci * lc, (ci + 1) * lc)
            gates, gates_t, bcum, bcum_t = per_chunk[ci]
            b_col = bcum[:, fl:fl + 1]
            b_row = bcum_t[fl:fl + 1, :]
            i_col = gates[:, h:h + 1]
            i_row = gates_t[h:h + 1, :]
            m_prev = m_scr[h:h + 1, 0:1]
            dmat = jnp.where(causal, b_col - b_row + i_row, -jnp.inf)
            inter = b_col + m_prev
            m_t = jnp.maximum(inter, jnp.max(dmat, axis=-1, keepdims=True))
            w_intra = jnp.exp(dmat - m_t)
            w_inter = jnp.exp(inter - m_t)
            qh = proj[rows, 0:ML_DK].astype(BF16)
            kh = proj[rows, ML_DK:2 * ML_DK]
            vh = proj[rows, 2 * ML_DK:2 * ML_DK + ML_DV].astype(BF16)
            og = proj[rows, 2 * ML_DK + ML_DV:2 * ML_DK + 2 * ML_DV]
            path = proj[rows, 2 * ML_DK + 2 * ML_DV:ML_HEAD_COLS]
            v_ext = jnp.concatenate([vh, ones], axis=1)
            qk = lax.dot_general(qh, kh.astype(BF16), (((1,), (1,)), ((), ())), preferred_element_type=F32)
            qk = qk * scale * w_intra
            state = c_scr[h]
            num_ext = (jnp.dot(qk.astype(BF16), v_ext, preferred_element_type=F32)
                       + w_inter * jnp.dot(qh, state.astype(BF16), preferred_element_type=F32))
            num = num_ext[:, :ML_DV]
            den = num_ext[:, ML_DV:]
            den = jnp.maximum(jnp.abs(den), jnp.exp(-m_t))
            hh = num / jnp.concatenate([den, den], axis=1)
            b_last = bcum[lc - 1:lc, fl:fl + 1]
            m_new = m_t[lc - 1:lc, :]
            w_state = jnp.exp(b_last - b_col + i_col - m_new)
            decay = jnp.exp(b_last + m_prev - m_new)
            kw_t = (kh * (scale * w_state)).T.astype(BF16)
            c_scr[h] = decay * state + jnp.dot(kw_t, v_ext, preferred_element_type=F32)
            m_scr[h:h + 1, :] = jnp.broadcast_to(m_new, (1, LANES))
            z_ref[rows, h * ML_DV:(h + 1) * ML_DV] = (_sigmoid(og) * hh * _silu(path)).astype(BF16)


def mlstm_proj(xb, w_ml, layer, bias_row, bsz, seq, tt=TT_MLSTM, lc=L_MLSTM):
    t, d = xb.shape
    nt = seq // tt
    return pl.pallas_call(
        functools.partial(_mlstm_proj_kernel, tt=tt, lc=lc),
        grid=(bsz, nt),
        in_specs=[
            pl.BlockSpec((tt, d), lambda b, c: (b * nt + c, 0)),
            pl.BlockSpec((None,) + w_ml.shape[1:], lambda b, c: (layer, 0, 0), pipeline_mode=pl.Buffered(1)),
            pl.BlockSpec((1, LANES), lambda b, c: (0, 0)),
        ],
        out_specs=pl.BlockSpec((tt, BRANCH_W), lambda b, c: (b * nt + c, 0)),
        out_shape=jax.ShapeDtypeStruct((t, BRANCH_W), BF16),
        scratch_shapes=[pltpu.VMEM((ML_HEADS, ML_DK, ML_DV + LANES), F32),
                        pltpu.VMEM((8, LANES), F32)],
        compiler_params=_cparams(("parallel", "arbitrary")),
        name="mlstm_proj",
    )(xb, w_ml, bias_row)


def mlstm_proj_weight(w16):
    pad = jnp.zeros(w16.shape[:-1] + (LANES - 2 * ML_HEADS,), w16.dtype)
    cols = [w16[..., O_MLI:O_MLO], pad]
    for h in range(ML_HEADS):
        cols += [w16[..., O_MLQ + h * ML_DK:O_MLQ + (h + 1) * ML_DK],
                 w16[..., O_MLK + h * ML_DK:O_MLK + (h + 1) * ML_DK],
                 w16[..., O_MLV + h * ML_DV:O_MLV + (h + 1) * ML_DV],
                 w16[..., O_MLO + h * ML_DV:O_MLO + (h + 1) * ML_DV],
                 w16[..., O_SILU + 3 * BRANCH_W + h * ML_DV:O_SILU + 3 * BRANCH_W + (h + 1) * ML_DV]]
    return jnp.concatenate(cols, axis=-1)


def _xattn_proj_kernel(x_ref, w_ref, kv_ref, o_ref):
    scale = XA_HD ** -0.5
    x = x_ref[...]
    for h in range(XA_HEADS):
        sl = slice(h * XA_HD, (h + 1) * XA_HD)
        proj = jnp.dot(x, w_ref[:, 2 * h * XA_HD:2 * (h + 1) * XA_HD], preferred_element_type=F32)
        qh = proj[:, :XA_HD].astype(BF16)
        path = proj[:, XA_HD:]
        kh = kv_ref[:, sl]
        vh = kv_ref[:, BRANCH_W + h * XA_HD:BRANCH_W + (h + 1) * XA_HD]
        s = lax.dot_general(qh, kh, (((1,), (1,)), ((), ())), preferred_element_type=F32) * scale
        mx = jnp.max(s, axis=-1, keepdims=True)
        p = jnp.exp(s - mx)
        den = jnp.sum(p, axis=-1, keepdims=True)
        o = jnp.dot(p.astype(BF16), vh, preferred_element_type=F32) / den
        o_ref[:, sl] = (o * _silu(path)).astype(BF16)


def cross_attention_proj(xb, w_xa, layer, kv, bsz, seq, mem_len, tq=TQ_XA):
    t, d = xb.shape
    nq = seq // tq
    return pl.pallas_call(
        _xattn_proj_kernel,
        grid=(bsz, nq),
        in_specs=[
            pl.BlockSpec((tq, d), lambda b, i: (b * nq + i, 0)),
            pl.BlockSpec((None,) + w_xa.shape[1:], lambda b, i: (layer, 0, 0), pipeline_mode=pl.Buffered(1)),
            pl.BlockSpec((mem_len, 2 * BRANCH_W), lambda b, i: (b, 0)),
        ],
        out_specs=pl.BlockSpec((tq, BRANCH_W), lambda b, i: (b * nq + i, 0)),
        out_shape=jax.ShapeDtypeStruct((t, BRANCH_W), BF16),
        compiler_params=_cparams(("parallel", "parallel")),
        name="cross_attn_proj",
    )(xb, w_xa, kv)


def xattn_proj_weight(w16):
    cols = []
    for h in range(XA_HEADS):
        cols += [w16[..., O_XAQ + h * XA_HD:O_XAQ + (h + 1) * XA_HD],
                 w16[..., O_SILU + 4 * BRANCH_W + h * XA_HD:O_SILU + 4 * BRANCH_W + (h + 1) * XA_HD]]
    return jnp.concatenate(cols, axis=-1)


def _merge_kernel(x_ref, za_ref, zb_ref, zc_ref, zd_ref, ze_ref, wg_ref, wb_ref, o_ref):
    x = x_ref[...]
    acc = None
    for n, z_ref in enumerate((za_ref, zb_ref, zc_ref, zd_ref, ze_ref)):
        gate = _sigmoid(jnp.dot(x, wg_ref[n], preferred_element_type=F32))
        term = gate * jnp.dot(z_ref[...], wb_ref[n], preferred_element_type=F32)
        acc = term if acc is None else acc + term
    o_ref[...] = acc.astype(BF16)


def gated_merge(xb, zs, wg, wb, layer, tm=TM_MERGE, tn=TN_MERGE):
    t, d = xb.shape
    tm = min(tm, t)
    w = BRANCH_W
    z_spec = pl.BlockSpec((tm, w), lambda i, j: (i, 0))
    return pl.pallas_call(
        _merge_kernel,
        grid=(t // tm, d // tn),
        in_specs=[pl.BlockSpec((tm, d), lambda i, j: (i, 0)),
                  z_spec, z_spec, z_spec, z_spec, z_spec,
                  pl.BlockSpec((None, N_BRANCH, d, tn), lambda i, j: (layer, 0, 0, j)),
                  pl.BlockSpec((None, N_BRANCH, w, tn), lambda i, j: (layer, 0, 0, j))],
        out_specs=pl.BlockSpec((tm, tn), lambda i, j: (i, j)),
        out_shape=jax.ShapeDtypeStruct((t, d), BF16),
        compiler_params=_cparams(("parallel", "parallel")),
        name="gated_merge",
    )(xb, *zs, wg, wb)


def _out_kernel(m_ref, w_ref, x_ref, g_ref, b_ref, of_ref, ob_ref):
    out = jnp.dot(m_ref[...], w_ref[...], preferred_element_type=F32)
    y = _ln_rows(DN_ALPHA * x_ref[...] + out, g_ref[...], b_ref[...])
    of_ref[...] = y
    ob_ref[...] = y.astype(BF16)


def out_proj_ln(merged, w_out, layer, x, g, b, tm=TM_OUT):
    t, d = x.shape
    tm = min(tm, t)
    row = pl.BlockSpec((tm, d), lambda i: (i, 0))
    vec = pl.BlockSpec((1, d), lambda i: (0, 0))
    return pl.pallas_call(
        _out_kernel,
        grid=(t // tm,),
        in_specs=[row,
                  pl.BlockSpec((None, d, d), lambda i: (layer, 0, 0), pipeline_mode=pl.Buffered(1)),
                  row, vec, vec],
        out_specs=[row, row],
        out_shape=[jax.ShapeDtypeStruct((t, d), F32), jax.ShapeDtypeStruct((t, d), BF16)],
        compiler_params=_cparams(("parallel",)),
        name="out_proj_ln",
    )(merged, w_out, x, g.reshape(1, d), b.reshape(1, d))


def prepare_weights(w_in, w_gate, w_bout, w_out, mem_wkv, ml_i_bias, ml_f_bias):
    w16 = w_in.astype(BF16)
    pad = LANES - 2 * ML_HEADS
    return dict(
        attn=jnp.concatenate([w16[..., O_AQ:O_AK], w16[..., O_SILU:O_SILU + BRANCH_W],
                              w16[..., O_AK:O_SCH]], axis=-1),
        conv=conv_proj_weight(w16),
        ml=mlstm_proj_weight(w16),
        xa=xattn_proj_weight(w16),
        bias=jnp.pad(jnp.concatenate([ml_i_bias, ml_f_bias], axis=-1).astype(F32), ((0, 0), (0, pad))),
        kv=mem_wkv.astype(BF16),
        wg=w_gate.astype(BF16),
        wb=w_bout.astype(BF16),
        wo=w_out.astype(BF16),
    )


def hybrid_layer(xf, xb, mem_b, bsz, seq, wts, l, sinks, sc_w, cf_w, cf_g, cf_b, ln_g, ln_b):
    mem_len = mem_b.shape[0] // bsz
    qkvp = matmul(xb, wts["attn"], l, BF16, TM_MM, wts["attn"].shape[-1], "proj_attn")
    kv = matmul(mem_b, wts["kv"], l, BF16, TM_MM, 1024, "mem_kv")

    z_a = window_attention(qkvp, sinks, bsz, seq)
    z_b, z_c = conv_mixers_proj(xb, wts["conv"], l, sc_w, cf_w, cf_g, cf_b, bsz, seq)
    z_d = mlstm_proj(xb, wts["ml"], l, wts["bias"][l].reshape(1, LANES), bsz, seq)
    z_e = cross_attention_proj(xb, wts["xa"], l, kv, bsz, seq, mem_len)

    merged = gated_merge(xb, (z_a, z_b, z_c, z_d, z_e), wts["wg"], wts["wb"], l)
    return out_proj_ln(merged, wts["wo"], l, xf, ln_g, ln_b)


def kernel(x, mem, ln_in_g, ln_in_b, w_in, w_gate, w_bout, w_out, attn_sinks, sc_w, cf_w, cf_ln_g,
           cf_ln_b, ml_i_bias, ml_f_bias, mem_wkv, ln_g, ln_b):
    bsz, seq, d = x.shape
    wts = prepare_weights(w_in, w_gate, w_bout, w_out, mem_wkv, ml_i_bias, ml_f_bias)
    xf, xb = layer_norm_in(x.reshape(bsz * seq, d), ln_in_g, ln_in_b)
    mem_b = mem.reshape(-1, d).astype(BF16)
    for l in range(w_in.shape[0]):
        xf, xb = hybrid_layer(xf, xb, mem_b, bsz, seq, wts, l, attn_sinks[l], sc_w[l], cf_w[l],
                              cf_ln_g[l], cf_ln_b[l], ln_g[l], ln_b[l])
    return xf.reshape(bsz, seq, d)
```

```python
import functools

import jax
import jax.numpy as jnp
import numpy as np
from jax import lax
from jax.experimental import pallas as pl
from jax.experimental.pallas import tpu as pltpu

F32 = jnp.float32
BF16 = jnp.bfloat16

D_MODEL = 2048
DEPTH = 4
CHUNK = 64
BRANCH_W = 1024
N_BRANCH = 5
A_HEADS = 16
A_KV_HEADS = 2
A_HD = 64
A_WINDOW = 128
SC_WIDTH = 3
CF_WIDTH = 31
ML_HEADS = 4
ML_DK = 128
ML_DV = 256
XA_HEADS = 4
XA_HD = BRANCH_W // XA_HEADS
DN_ALPHA = (2 * DEPTH) ** 0.25
LN_EPS = 1e-5

IN_SPLITS = (
    A_HEADS * A_HD, A_KV_HEADS * A_HD, A_KV_HEADS * A_HD,
    BRANCH_W, BRANCH_W, BRANCH_W,
    BRANCH_W, BRANCH_W,
    ML_HEADS * ML_DK, ML_HEADS * ML_DK, ML_HEADS * ML_DV,
    ML_HEADS, ML_HEADS, BRANCH_W,
    BRANCH_W,
    N_BRANCH * BRANCH_W,
)
_OFF = tuple(int(v) for v in np.cumsum((0,) + IN_SPLITS))
(O_AQ, O_AK, O_AV, O_SCH, O_SCBG, O_SCCG, O_CFV, O_CFG, O_MLQ, O_MLK, O_MLV,
 O_MLI, O_MLF, O_MLO, O_XAQ, O_SILU, O_END) = _OFF

LANES = 128
HALO = 32
ALIBI_SLOPES = tuple(float(2.0 ** (-8.0 * (h + 1) / A_HEADS)) for h in range(A_HEADS))

TM_MM = 1024
TQ_ATTN = 1024
TS_CONV = 256
L_MLSTM = 128
TT_MLSTM = 256
TQ_XA = 512
TM_MERGE = 1024
TN_MERGE = 256
TM_OUT = 512
VMEM_LIMIT = 56 * 1024 * 1024


def _cparams(sem):
    return pltpu.CompilerParams(dimension_semantics=sem, vmem_limit_bytes=VMEM_LIMIT)


def _sigmoid(x):
    return 1.0 / (1.0 + jnp.exp(-x))


def _silu(x):
    return x * _sigmoid(x)


def _ln_rows(x, g, b):
    mu = jnp.mean(x, axis=-1, keepdims=True)
    xc = x - mu
    var = jnp.mean(xc * xc, axis=-1, keepdims=True)
    return xc * lax.rsqrt(var + LN_EPS) * g + b


def _ln_kernel(x_ref, g_ref, b_ref, of_ref, ob_ref):
    y = _ln_rows(x_ref[...], g_ref[...], b_ref[...])
    of_ref[...] = y
    ob_ref[...] = y.astype(BF16)


def layer_norm_in(x2d, g, b, tm=512):
    t, d = x2d.shape
    return pl.pallas_call(
        _ln_kernel,
        grid=(t // tm,),
        in_specs=[pl.BlockSpec((tm, d), lambda i: (i, 0)),
                  pl.BlockSpec((1, d), lambda i: (0, 0)),
                  pl.BlockSpec((1, d), lambda i: (0, 0))],
        out_specs=[pl.BlockSpec((tm, d), lambda i: (i, 0)),
                   pl.BlockSpec((tm, d), lambda i: (i, 0))],
        out_shape=[jax.ShapeDtypeStruct((t, d), F32), jax.ShapeDtypeStruct((t, d), BF16)],
        compiler_params=_cparams(("parallel",)),
        name="ln_in",
    )(x2d, g.reshape(1, d), b.reshape(1, d))


def _mm_kernel(a_ref, b_ref, o_ref):
    o_ref[...] = jnp.dot(a_ref[...], b_ref[...], preferred_element_type=F32).astype(o_ref.dtype)


def matmul(a, b, layer, out_dtype, tm, tn, name):
    m, k = a.shape
    n = b.shape[2]
    tm = min(tm, m)
    return pl.pallas_call(
        _mm_kernel,
        grid=(m // tm, n // tn),
        in_specs=[pl.BlockSpec((tm, k), lambda i, j: (i, 0)),
                  pl.BlockSpec((None, k, tn), lambda i, j: (layer, 0, j))],
        out_specs=pl.BlockSpec((tm, tn), lambda i, j: (i, j)),
        out_shape=jax.ShapeDtypeStruct((m, n), out_dtype),
        compiler_params=_cparams(("parallel", "parallel")),
        name=name,
    )(a, b)


QB_ATTN = A_WINDOW


def _attn_kernel(q_ref, kvc_ref, kvp_ref, path_ref, snk_ref, bias_ref, o_ref, *, tq):
    i = pl.program_id(1)
    nk = A_WINDOW + tq
    kv = jnp.concatenate([kvp_ref[...], kvc_ref[...]], axis=0)
    low = lax.broadcasted_iota(jnp.int32, (nk, LANES), 1) < A_HD
    low_q = lax.broadcasted_iota(jnp.int32, (QB_ATTN, LANES), 1) < A_HD
    zero = jnp.zeros((nk, LANES), BF16)

    def halves(blk):
        swapped = jnp.concatenate([blk[:, A_HD:], blk[:, :A_HD]], axis=1)
        return {(0, 0): jnp.where(low, blk, zero), (0, 1): jnp.where(low, zero, swapped),
                (1, 0): jnp.where(low, swapped, zero), (1, 1): jnp.where(low, zero, blk)}

    k_half = halves(kv[:, :LANES] * (A_HD ** -0.5))
    v_half = halves(kv[:, LANES:])
    for r in range(tq // QB_ATTN):
        rows = slice(r * QB_ATTN, (r + 1) * QB_ATTN)
        keys = slice(r * QB_ATTN, r * QB_ATTN + 2 * A_WINDOW)
        tbl = jnp.where(i == 0, 0, 1) if r == 0 else 1
        for g in range(A_KV_HEADS):
            pairs = range(g * ATTN_PAIRS, (g + 1) * ATTN_PAIRS)
            q_stack = jnp.concatenate([q_ref[rows, hp * LANES:(hp + 1) * LANES] for hp in pairs], axis=0)
            o_stack = None
            rden = []
            for par in range(2):
                s = lax.dot_general(q_stack, k_half[g, par][keys], (((1,), (1,)), ((), ())),
                                    preferred_element_type=F32)
                s = s + bias_ref[tbl, g, par]
                snk = jnp.concatenate(
                    [jnp.broadcast_to(snk_ref[2 * hp + par:2 * hp + par + 1, 0:1], (QB_ATTN, 1)) for hp in pairs],
                    axis=0)
                mx = jnp.maximum(jnp.max(s, axis=-1, keepdims=True), snk)
                p = jnp.exp(s - mx)
                den = jnp.sum(p, axis=-1, keepdims=True) + jnp.exp(snk - mx)
                o = jnp.dot(p.astype(BF16), v_half[g, par][keys], preferred_element_type=F32)
                o_stack = o if o_stack is None else o_stack + o
                rden.append(1.0 / den)
            for j, hp in enumerate(pairs):
                blk = slice(j * QB_ATTN, (j + 1) * QB_ATTN)
                y = o_stack[blk] * jnp.where(low_q, rden[0][blk], rden[1][blk])
                path = path_ref[rows, hp * LANES:(hp + 1) * LANES].astype(F32)
                o_ref[rows, hp * LANES:(hp + 1) * LANES] = (y * _silu(path)).astype(BF16)


ATTN_PAIRS = A_HEADS // A_KV_HEADS // 2


def attn_bias_tables():
    t = np.arange(QB_ATTN)[:, None]
    s = np.arange(2 * A_WINDOW)[None, :]
    q_chunk = t // CHUNK + A_WINDOW // CHUNK
    k_chunk = s // CHUNK
    valid = (k_chunk <= q_chunk) & (k_chunk >= q_chunk - A_WINDOW // CHUNK)
    dist = np.abs(t + A_WINDOW - s).astype(np.float32)
    slopes = np.asarray(ALIBI_SLOPES, np.float32)[:, None, None]
    general = np.where(valid[None], -slopes * dist[None], -np.inf)
    first = np.where((valid & (s >= A_WINDOW))[None], -slopes * dist[None], -np.inf)
    per_head = np.stack([first, general]).astype(np.float32)
    tab = per_head.reshape(2, A_KV_HEADS, ATTN_PAIRS, 2, QB_ATTN, 2 * A_WINDOW).transpose(0, 1, 3, 2, 4, 5)
    return jnp.asarray(tab.reshape(2, A_KV_HEADS, 2, ATTN_PAIRS * QB_ATTN, 2 * A_WINDOW))


def window_attention(qpkv, sinks, bsz, seq, tq=TQ_ATTN):
    t = bsz * seq
    assert seq % tq == 0
    nq = seq // tq
    kvw = 2 * A_KV_HEADS * A_HD
    kv_col = (A_HEADS * A_HD + BRANCH_W) // kvw
    snk = jnp.broadcast_to(sinks.astype(F32)[:, None], (A_HEADS, LANES))
    bias = attn_bias_tables()
    return pl.pallas_call(
        functools.partial(_attn_kernel, tq=tq),
        grid=(bsz, nq),
        in_specs=[
            pl.BlockSpec((tq, A_HEADS * A_HD), lambda b, i: (b * nq + i, 0)),
            pl.BlockSpec((tq, kvw), lambda b, i: (b * nq + i, kv_col)),
            pl.BlockSpec((A_WINDOW, kvw),
                         lambda b, i: (jnp.maximum((b * nq + i) * (tq // A_WINDOW) - 1, 0), kv_col)),
            pl.BlockSpec((tq, BRANCH_W), lambda b, i: (b * nq + i, 1)),
            pl.BlockSpec((A_HEADS, LANES), lambda b, i: (0, 0)),
            pl.BlockSpec(bias.shape, lambda b, i: (0,) * bias.ndim),
        ],
        out_specs=pl.BlockSpec((tq, BRANCH_W), lambda b, i: (b * nq + i, 0)),
        out_shape=jax.ShapeDtypeStruct((t, BRANCH_W), BF16),
        compiler_params=_cparams(("parallel", "parallel")),
        name="window_attn",
    )(qpkv, qpkv, qpkv, qpkv, snk, bias)


CV_SLAB = 256
CV_GROUPS = 7


def _conv_proj_kernel(x_ref, w_ref, scw_ref, cfw_ref, lng_ref, lnb_ref, zb_ref, zc_ref,
                      ub_scr, uc_scr, cv_scr, pc_scr, *, ts):
    i = pl.program_id(1)
    n_sub = BRANCH_W // LANES

    @pl.when(i == 0)
    def _():
        ub_scr[:, 0:HALO, :] = jnp.zeros((n_sub, HALO, LANES), F32)
        uc_scr[:, 0:HALO, :] = jnp.zeros((n_sub, HALO, LANES), F32)

    x = x_ref[...]
    for s4 in range(BRANCH_W // CV_SLAB):
        c0 = s4 * CV_GROUPS * CV_SLAB
        proj = jnp.dot(x, w_ref[:, c0:c0 + CV_GROUPS * CV_SLAB], preferred_element_type=F32)
        for half in range(CV_SLAB // LANES):
            s = s4 * (CV_SLAB // LANES) + half
            ln = slice(s * LANES, (s + 1) * LANES)

            def grp(g, half=half, proj=proj):
                return proj[:, g * CV_SLAB + half * LANES:g * CV_SLAB + (half + 1) * LANES]

            ub_scr[s, HALO:HALO + ts, :] = grp(2) * grp(0)
            uc_scr[s, HALO:HALO + ts, :] = grp(3) * _sigmoid(grp(4))
            acc = jnp.zeros((ts, LANES), F32)
            for j in range(SC_WIDTH):
                win = ub_scr[s, pl.ds(HALO - (SC_WIDTH - 1) + j, ts, stride=1), :]
                acc = acc + scw_ref[j:j + 1, ln] * win
            zb_ref[:, ln] = (grp(1) * acc * _silu(grp(5))).astype(BF16)
            acc = jnp.zeros((ts, LANES), F32)
            for j in range(CF_WIDTH):
                win = uc_scr[s, pl.ds(HALO - (CF_WIDTH - 1) + j, ts, stride=1), :]
                acc = acc + cfw_ref[j:j + 1, ln] * win
            cv_scr[:, ln] = acc
            pc_scr[:, ln] = _silu(grp(6))
            ub_scr[s, 0:HALO, :] = ub_scr[s, ts:ts + HALO, :]
            uc_scr[s, 0:HALO, :] = uc_scr[s, ts:ts + HALO, :]

    yc = _silu(_ln_rows(cv_scr[...], lng_ref[...], lnb_ref[...]))
    zc_ref[...] = (yc * pc_scr[...]).astype(BF16)


def conv_mixers_proj(xb, w_cv, layer, sc_w, cf_w, cf_g, cf_b, bsz, seq, ts=TS_CONV):
    t, d = xb.shape
    assert seq % ts == 0
    ns = seq // ts
    w = BRANCH_W
    n_sub = w // LANES

    def full(shape):
        return pl.BlockSpec(shape, lambda b, i: (0, 0))

    out_spec = pl.BlockSpec((ts, w), lambda b, i: (b * ns + i, 0))
    return pl.pallas_call(
        functools.partial(_conv_proj_kernel, ts=ts),
        grid=(bsz, ns),
        in_specs=[pl.BlockSpec((ts, d), lambda b, i: (b * ns + i, 0)),
                  pl.BlockSpec((None,) + w_cv.shape[1:], lambda b, i: (layer, 0, 0),
                               pipeline_mode=pl.Buffered(1)),
                  full((SC_WIDTH, w)), full((CF_WIDTH, w)), full((1, w)), full((1, w))],
        out_specs=[out_spec, out_spec],
        out_shape=[jax.ShapeDtypeStruct((t, w), BF16), jax.ShapeDtypeStruct((t, w), BF16)],
        scratch_shapes=[pltpu.VMEM((n_sub, HALO + ts, LANES), F32),
                        pltpu.VMEM((n_sub, HALO + ts, LANES), F32),
                        pltpu.VMEM((ts, w), F32), pltpu.VMEM((ts, w), F32)],
        compiler_params=_cparams(("parallel", "arbitrary")),
        name="conv_mixers_proj",
    )(xb, w_cv, sc_w.astype(F32), cf_w.astype(F32), cf_g.reshape(1, w).astype(F32), cf_b.reshape(1, w).astype(F32))


def _log_sigmoid(x):
    return jnp.minimum(x, 0.0) - jnp.log(1.0 + jnp.exp(-jnp.abs(x)))


ML_HEAD_COLS = 2 * ML_DK + 3 * ML_DV


def _mlstm_proj_kernel(x_ref, w_ref, bias_ref, z_ref, c_scr, m_scr, *, tt, lc):
    c = pl.program_id(1)

    @pl.when(c == 0)
    def _():
        c_scr[...] = jnp.zeros_like(c_scr)
        m_scr[...] = jnp.zeros_like(m_scr)

    x = x_ref[...]
    gates_all = jnp.dot(x, w_ref[:, 0:LANES], preferred_element_type=F32) + bias_ref[...]
    row = lax.broadcasted_iota(jnp.int32, (lc, lc), 0)
    col = lax.broadcasted_iota(jnp.int32, (lc, lc), 1)
    causal = col <= row
    tri = causal.astype(F32)
    n_chunks = tt // lc
    per_chunk = []
    for ci in range(n_chunks):
        gates = gates_all[ci * lc:(ci + 1) * lc]
        bcum = jnp.dot(tri, _log_sigmoid(gates), preferred_element_type=F32,
                       precision=lax.Precision.HIGHEST)
        per_chunk.append((gates, gates.T, bcum, bcum.T))
    ones = jnp.ones((lc, LANES), BF16)
    scale = ML_DK ** -0.5
    for h in range(ML_HEADS):
        c0 = LANES + h * ML_HEAD_COLS
        proj = jnp.dot(x, w_ref[:, c0:c0 + ML_HEAD_COLS], preferred_element_type=F32)
        fl = ML_HEADS + h
        for ci in range(n_chunks):
            rows = slice(ci * lc, (ci + 1) * lc)
            gates, gates_t, bcum, bcum_t = per_chunk[ci]
            b_col = bcum[:, fl:fl + 1]
            b_row = bcum_t[fl:fl + 1, :]
            i_col = gates[:, h:h + 1]
            i_row = gates_t[h:h + 1, :]
            m_prev = m_scr[h:h + 1, 0:1]
            dmat = jnp.where(causal, b_col - b_row + i_row, -jnp.inf)
            inter = b_col + m_prev
            m_t = jnp.maximum(inter, jnp.max(dmat, axis=-1, keepdims=True))
            w_intra = jnp.exp(dmat - m_t)
            w_inter = jnp.exp(inter - m_t)
            qh = proj[rows, 0:ML_DK].astype(BF16)
            kh = proj[rows, ML_DK:2 * ML_DK]
            vh = proj[rows, 2 * ML_DK:2 * ML_DK + ML_DV].astype(BF16)
            og = proj[rows, 2 * ML_DK + ML_DV:2 * ML_DK + 2 * ML_DV]
            path = proj[rows, 2 * ML_DK + 2 * ML_DV:ML_HEAD_COLS]
            v_ext = jnp.concatenate([vh, ones], axis=1)
            qk = lax.dot_general(qh, kh.astype(BF16), (((1,), (1,)), ((), ())), preferred_element_type=F32)
            qk = qk * scale * w_intra
            state = c_scr[h]
            num_ext = (jnp.dot(qk.astype(BF16), v_ext, preferred_element_type=F32)
                       + w_inter * jnp.dot(qh, state.astype(BF16), preferred_element_type=F32))
            num = num_ext[:, :ML_DV]
            den = num_ext[:, ML_DV:]
            den = jnp.maximum(jnp.abs(den), jnp.exp(-m_t))
            hh = num / jnp.concatenate([den, den], axis=1)
            b_last = bcum[lc - 1:lc, fl:fl + 1]
            m_new = m_t[lc - 1:lc, :]
            w_state = jnp.exp(b_last - b_col + i_col - m_new)
            decay = jnp.exp(b_last + m_prev - m_new)
            kw_t = (kh * (scale * w_state)).T.astype(BF16)
            c_scr[h] = decay * state + jnp.dot(kw_t, v_ext, preferred_element_type=F32)
            m_scr[h:h + 1, :] = jnp.broadcast_to(m_new, (1, LANES))
            z_ref[rows, h * ML_DV:(h + 1) * ML_DV] = (_sigmoid(og) * hh * _silu(path)).astype(BF16)


def mlstm_proj(xb, w_ml, layer, bias_row, bsz, seq, tt=TT_MLSTM, lc=L_MLSTM):
    t, d = xb.shape
    assert seq % tt == 0 and tt % lc == 0
    nt = seq // tt
    return pl.pallas_call(
        functools.partial(_mlstm_proj_kernel, tt=tt, lc=lc),
        grid=(bsz, nt),
        in_specs=[
            pl.BlockSpec((tt, d), lambda b, c: (b * nt + c, 0)),
            pl.BlockSpec((None,) + w_ml.shape[1:], lambda b, c: (layer, 0, 0), pipeline_mode=pl.Buffered(1)),
            pl.BlockSpec((1, LANES), lambda b, c: (0, 0)),
        ],
        out_specs=pl.BlockSpec((tt, BRANCH_W), lambda b, c: (b * nt + c, 0)),
        out_shape=jax.ShapeDtypeStruct((t, BRANCH_W), BF16),
        scratch_shapes=[pltpu.VMEM((ML_HEADS, ML_DK, ML_DV + LANES), F32),
                        pltpu.VMEM((8, LANES), F32)],
        compiler_params=_cparams(("parallel", "arbitrary")),
        name="mlstm_proj",
    )(xb, w_ml, bias_row)


def _xattn_proj_kernel(x_ref, w_ref, kv_ref, o_ref):
    scale = XA_HD ** -0.5
    x = x_ref[...]
    for h in range(XA_HEADS):
        sl = slice(h * XA_HD, (h + 1) * XA_HD)
        proj = jnp.dot(x, w_ref[:, 2 * h * XA_HD:2 * (h + 1) * XA_HD], preferred_element_type=F32)
        qh = proj[:, :XA_HD].astype(BF16)
        path = proj[:, XA_HD:]
        kh = kv_ref[:, sl]
        vh = kv_ref[:, BRANCH_W + h * XA_HD:BRANCH_W + (h + 1) * XA_HD]
        s = lax.dot_general(qh, kh, (((1,), (1,)), ((), ())), preferred_element_type=F32) * scale
        mx = jnp.max(s, axis=-1, keepdims=True)
        p = jnp.exp(s - mx)
        den = jnp.sum(p, axis=-1, keepdims=True)
        o = jnp.dot(p.astype(BF16), vh, preferred_element_type=F32) / den
        o_ref[:, sl] = (o * _silu(path)).astype(BF16)


def cross_attention_proj(xb, w_xa, layer, kv, bsz, seq, mem_len, tq=TQ_XA):
    t, d = xb.shape
    assert seq % tq == 0
    nq = seq // tq
    return pl.pallas_call(
        _xattn_proj_kernel,
        grid=(bsz, nq),
        in_specs=[
            pl.BlockSpec((tq, d), lambda b, i: (b * nq + i, 0)),
            pl.BlockSpec((None,) + w_xa.shape[1:], lambda b, i: (layer, 0, 0), pipeline_mode=pl.Buffered(1)),
            pl.BlockSpec((mem_len, 2 * BRANCH_W), lambda b, i: (b, 0)),
        ],
        out_specs=pl.BlockSpec((tq, BRANCH_W), lambda b, i: (b * nq + i, 0)),
        out_shape=jax.ShapeDtypeStruct((t, BRANCH_W), BF16),
        compiler_params=_cparams(("parallel", "parallel")),
        name="cross_attn_proj",
    )(xb, w_xa, kv)


def _merge_kernel(x_ref, za_ref, zb_ref, zc_ref, zd_ref, ze_ref, wg_ref, wb_ref, o_ref):
    x = x_ref[...]
    acc = None
    for n, z_ref in enumerate((za_ref, zb_ref, zc_ref, zd_ref, ze_ref)):
        gate = _sigmoid(jnp.dot(x, wg_ref[n], preferred_element_type=F32))
        term = gate * jnp.dot(z_ref[...], wb_ref[n], preferred_element_type=F32)
        acc = term if acc is None else acc + term
    o_ref[...] = acc.astype(BF16)


def gated_merge(xb, zs, wg, wb, layer, tm=TM_MERGE, tn=TN_MERGE):
    t, d = xb.shape
    tm = min(tm, t)
    w = BRANCH_W
    z_spec = pl.BlockSpec((tm, w), lambda i, j: (i, 0))
    return pl.pallas_call(
        _merge_kernel,
        grid=(t // tm, d // tn),
        in_specs=[pl.BlockSpec((tm, d), lambda i, j: (i, 0)),
                  z_spec, z_spec, z_spec, z_spec, z_spec,
                  pl.BlockSpec((None, N_BRANCH, d, tn), lambda i, j: (layer, 0, 0, j)),
                  pl.BlockSpec((None, N_BRANCH, w, tn), lambda i, j: (layer, 0, 0, j))],
        out_specs=pl.BlockSpec((tm, tn), lambda i, j: (i, j)),
        out_shape=jax.ShapeDtypeStruct((t, d), BF16),
        compiler_params=_cparams(("parallel", "parallel")),
        name="gated_merge",
    )(xb, *zs, wg, wb)


def _out_kernel(m_ref, w_ref, x_ref, g_ref, b_ref, of_ref, ob_ref):
    out = jnp.dot(m_ref[...], w_ref[...], preferred_element_type=F32)
    y = _ln_rows(DN_ALPHA * x_ref[...] + out, g_ref[...], b_ref[...])
    of_ref[...] = y
    ob_ref[...] = y.astype(BF16)


def out_proj_ln(merged, w_out, layer, x, g, b, tm=TM_OUT):
    t, d = x.shape
    tm = min(tm, t)
    row = pl.BlockSpec((tm, d), lambda i: (i, 0))
    vec = pl.BlockSpec((1, d), lambda i: (0, 0))
    return pl.pallas_call(
        _out_kernel,
        grid=(t // tm,),
        in_specs=[row,
                  pl.BlockSpec((None, d, d), lambda i: (layer, 0, 0), pipeline_mode=pl.Buffered(1)),
                  row, vec, vec],
        out_specs=[row, row],
        out_shape=[jax.ShapeDtypeStruct((t, d), F32), jax.ShapeDtypeStruct((t, d), BF16)],
        compiler_params=_cparams(("parallel",)),
        name="out_proj_ln",
    )(merged, w_out, x, g.reshape(1, d), b.reshape(1, d))


def _regroup_segments():
    silu = lambda n: O_SILU + n * BRANCH_W
    attn = [(O_AQ, 0, BRANCH_W), (silu(0), BRANCH_W, BRANCH_W), (O_AK, 2 * BRANCH_W, O_SCH - O_AK)]
    conv = []
    for s in range(BRANCH_W // CV_SLAB):
        for gi, o in enumerate((O_SCH, O_SCBG, O_SCCG, O_CFV, O_CFG, silu(1), silu(2))):
            conv.append((o + s * CV_SLAB, (s * CV_GROUPS + gi) * CV_SLAB, CV_SLAB))
    ml = []
    for h in range(ML_HEADS):
        d0 = LANES + h * ML_HEAD_COLS
        ml += [(O_MLQ + h * ML_DK, d0, ML_DK), (O_MLK + h * ML_DK, d0 + ML_DK, ML_DK),
               (O_MLV + h * ML_DV, d0 + 2 * ML_DK, ML_DV), (O_MLO + h * ML_DV, d0 + 2 * ML_DK + ML_DV, ML_DV),
               (silu(3) + h * ML_DV, d0 + 2 * ML_DK + 2 * ML_DV, ML_DV)]
    xa = []
    for h in range(XA_HEADS):
        xa += [(O_XAQ + h * XA_HD, 2 * h * XA_HD, XA_HD), (silu(4) + h * XA_HD, (2 * h + 1) * XA_HD, XA_HD)]
    return attn, conv, ml, xa


def _regroup_kernel(w_ref, attn_ref, conv_ref, ml_ref, xa_ref):
    for dst, segs in zip((attn_ref, conv_ref, ml_ref, xa_ref), _regroup_segments()):
        for src, d0, width in segs:
            dst[:, d0:d0 + width] = w_ref[:, src:src + width].astype(BF16)
    tile = w_ref[:, O_MLI:O_MLI + LANES]
    lane = lax.broadcasted_iota(jnp.int32, tile.shape, 1)
    ml_ref[:, 0:LANES] = jnp.where(lane < 2 * ML_HEADS, tile, 0.0).astype(BF16)


def regroup_w_in(w_in, rb=128):
    depth, d, n_in = w_in.shape
    widths = (2 * BRANCH_W + O_SCH - O_AK, CV_GROUPS * BRANCH_W, LANES + ML_HEADS * ML_HEAD_COLS, 2 * BRANCH_W)
    return pl.pallas_call(
        _regroup_kernel,
        grid=(depth, d // rb),
        in_specs=[pl.BlockSpec((None, rb, n_in), lambda l, r: (l, r, 0))],
        out_specs=[pl.BlockSpec((None, rb, w), lambda l, r: (l, r, 0)) for w in widths],
        out_shape=[jax.ShapeDtypeStruct((depth, d, w), BF16) for w in widths],
        compiler_params=_cparams(("parallel", "parallel")),
        name="regroup_w_in",
    )(w_in)


def prepare_weights(w_in, w_gate, w_bout, w_out, mem_wkv, ml_i_bias, ml_f_bias):
    attn, conv, ml, xa = regroup_w_in(w_in)
    pad = LANES - 2 * ML_HEADS
    return dict(
        attn=attn, conv=conv, ml=ml, xa=xa,
        bias=jnp.pad(jnp.concatenate([ml_i_bias, ml_f_bias], axis=-1).astype(F32), ((0, 0), (0, pad))),
        kv=mem_wkv.astype(BF16),
        wg=w_gate.astype(BF16),
        wb=w_bout.astype(BF16),
        wo=w_out.astype(BF16),
    )


def hybrid_layer(xf, xb, mem_b, bsz, seq, wts, l, sinks, sc_w, cf_w, cf_g, cf_b, ln_g, ln_b):
    mem_len = mem_b.shape[0] // bsz
    qkvp = matmul(xb, wts["attn"], l, BF16, TM_MM, wts["attn"].shape[-1], "proj_attn")
    kv = matmul(mem_b, wts["kv"], l, BF16, TM_MM, 1024, "mem_kv")

    z_a = window_attention(qkvp, sinks, bsz, seq)
    z_b, z_c = conv_mixers_proj(xb, wts["conv"], l, sc_w, cf_w, cf_g, cf_b, bsz, seq)
    z_d = mlstm_proj(xb, wts["ml"], l, wts["bias"][l].reshape(1, LANES), bsz, seq)
    z_e = cross_attention_proj(xb, wts["xa"], l, kv, bsz, seq, mem_len)

    merged = gated_merge(xb, (z_a, z_b, z_c, z_d, z_e), wts["wg"], wts["wb"], l)
    return out_proj_ln(merged, wts["wo"], l, xf, ln_g, ln_b)


def kernel(x, mem, ln_in_g, ln_in_b, w_in, w_gate, w_bout, w_out, attn_sinks, sc_w, cf_w, cf_ln_g,
           cf_ln_b, ml_i_bias, ml_f_bias, mem_wkv, ln_g, ln_b):
    bsz, seq, d = x.shape
    wts = prepare_weights(w_in, w_gate, w_bout, w_out, mem_wkv, ml_i_bias, ml_f_bias)
    xf, xb = layer_norm_in(x.reshape(bsz * seq, d), ln_in_g, ln_in_b)
    mem_b = mem.reshape(-1, d).astype(BF16)
    for l in range(w_in.shape[0]):
        xf, xb = hybrid_layer(xf, xb, mem_b, bsz, seq, wts, l, attn_sinks[l], sc_w[l], cf_w[l],
                              cf_ln_g[l], cf_ln_b[l], ln_g[l], ln_b[l])
    return xf.reshape(bsz, seq, d)
```

```python
import functools

import jax
import jax.numpy as jnp
import numpy as np
from jax import lax
from jax.experimental import pallas as pl
from jax.experimental.pallas import tpu as pltpu

F32 = jnp.float32
BF16 = jnp.bfloat16

D_MODEL = 2048
DEPTH = 4
CHUNK = 64
BRANCH_W = 1024
N_BRANCH = 5
A_HEADS = 16
A_KV_HEADS = 2
A_HD = 64
A_WINDOW = 128
SC_WIDTH = 3
CF_WIDTH = 31
ML_HEADS = 4
ML_DK = 128
ML_DV = 256
XA_HEADS = 4
XA_HD = BRANCH_W // XA_HEADS
DN_ALPHA = (2 * DEPTH) ** 0.25
LN_EPS = 1e-5

IN_SPLITS = (
    A_HEADS * A_HD, A_KV_HEADS * A_HD, A_KV_HEADS * A_HD,
    BRANCH_W, BRANCH_W, BRANCH_W,
    BRANCH_W, BRANCH_W,
    ML_HEADS * ML_DK, ML_HEADS * ML_DK, ML_HEADS * ML_DV,
    ML_HEADS, ML_HEADS, BRANCH_W,
    BRANCH_W,
    N_BRANCH * BRANCH_W,
)
_OFF = tuple(int(v) for v in np.cumsum((0,) + IN_SPLITS))
(O_AQ, O_AK, O_AV, O_SCH, O_SCBG, O_SCCG, O_CFV, O_CFG, O_MLQ, O_MLK, O_MLV,
 O_MLI, O_MLF, O_MLO, O_XAQ, O_SILU, O_END) = _OFF

LANES = 128
HALO = 32
ALIBI_SLOPES = tuple(float(2.0 ** (-8.0 * (h + 1) / A_HEADS)) for h in range(A_HEADS))

TM_MM = 1024
TQ_ATTN = 1024
TS_CONV = 256
L_MLSTM = 256
TT_MLSTM = 256
TQ_XA = 512
TM_MERGE = 1024
TN_MERGE = 256
TM_OUT = 512
VMEM_LIMIT = 56 * 1024 * 1024


def _cparams(sem):
    return pltpu.CompilerParams(dimension_semantics=sem, vmem_limit_bytes=VMEM_LIMIT)


def _sigmoid(x):
    return 1.0 / (1.0 + jnp.exp(-x))


def _silu(x):
    return x * _sigmoid(x)


def _ln_rows(x, g, b):
    mu = jnp.mean(x, axis=-1, keepdims=True)
    xc = x - mu
    var = jnp.mean(xc * xc, axis=-1, keepdims=True)
    return xc * lax.rsqrt(var + LN_EPS) * g + b


def _ln_kernel(x_ref, g_ref, b_ref, of_ref, ob_ref):
    y = _ln_rows(x_ref[...], g_ref[...], b_ref[...])
    of_ref[...] = y
    ob_ref[...] = y.astype(BF16)


def layer_norm_in(x2d, g, b, tm=512):
    t, d = x2d.shape
    return pl.pallas_call(
        _ln_kernel,
        grid=(t // tm,),
        in_specs=[pl.BlockSpec((tm, d), lambda i: (i, 0)),
                  pl.BlockSpec((1, d), lambda i: (0, 0)),
                  pl.BlockSpec((1, d), lambda i: (0, 0))],
        out_specs=[pl.BlockSpec((tm, d), lambda i: (i, 0)),
                   pl.BlockSpec((tm, d), lambda i: (i, 0))],
        out_shape=[jax.ShapeDtypeStruct((t, d), F32), jax.ShapeDtypeStruct((t, d), BF16)],
        compiler_params=_cparams(("parallel",)),
        name="ln_in",
    )(x2d, g.reshape(1, d), b.reshape(1, d))


def _mm_kernel(a_ref, b_ref, o_ref):
    o_ref[...] = jnp.dot(a_ref[...], b_ref[...], preferred_element_type=F32).astype(o_ref.dtype)


def matmul(a, b, layer, out_dtype, tm, tn, name):
    m, k = a.shape
    n = b.shape[2]
    tm = min(tm, m)
    return pl.pallas_call(
        _mm_kernel,
        grid=(m // tm, n // tn),
        in_specs=[pl.BlockSpec((tm, k), lambda i, j: (i, 0)),
                  pl.BlockSpec((None, k, tn), lambda i, j: (layer, 0, j))],
        out_specs=pl.BlockSpec((tm, tn), lambda i, j: (i, j)),
        out_shape=jax.ShapeDtypeStruct((m, n), out_dtype),
        compiler_params=_cparams(("parallel", "parallel")),
        name=name,
    )(a, b)


QB_ATTN = A_WINDOW


def _attn_kernel(q_ref, kvc_ref, kvp_ref, path_ref, snk_ref, bias_ref, o_ref, *, tq):
    i = pl.program_id(1)
    nk = A_WINDOW + tq
    kv = jnp.concatenate([kvp_ref[...], kvc_ref[...]], axis=0)
    low = lax.broadcasted_iota(jnp.int32, (nk, LANES), 1) < A_HD
    low_q = lax.broadcasted_iota(jnp.int32, (QB_ATTN, LANES), 1) < A_HD
    zero = jnp.zeros((nk, LANES), BF16)

    def halves(blk):
        swapped = jnp.concatenate([blk[:, A_HD:], blk[:, :A_HD]], axis=1)
        return {(0, 0): jnp.where(low, blk, zero), (0, 1): jnp.where(low, zero, swapped),
                (1, 0): jnp.where(low, swapped, zero), (1, 1): jnp.where(low, zero, blk)}

    k_half = halves(kv[:, :LANES] * (A_HD ** -0.5))
    v_half = halves(kv[:, LANES:])
    for r in range(tq // QB_ATTN):
        rows = slice(r * QB_ATTN, (r + 1) * QB_ATTN)
        keys = slice(r * QB_ATTN, r * QB_ATTN + 2 * A_WINDOW)
        tbl = jnp.where(i == 0, 0, 1) if r == 0 else 1
        for g in range(A_KV_HEADS):
            pairs = range(g * ATTN_PAIRS, (g + 1) * ATTN_PAIRS)
            q_stack = jnp.concatenate([q_ref[rows, hp * LANES:(hp + 1) * LANES] for hp in pairs], axis=0)
            o_stack = None
            rden = []
            for par in range(2):
                s = lax.dot_general(q_stack, k_half[g, par][keys], (((1,), (1,)), ((), ())),
                                    preferred_element_type=F32)
                s = s + bias_ref[tbl, g, par]
                snk = jnp.concatenate(
                    [jnp.broadcast_to(snk_ref[2 * hp + par:2 * hp + par + 1, 0:1], (QB_ATTN, 1)) for hp in pairs],
                    axis=0)
                mx = jnp.maximum(jnp.max(s, axis=-1, keepdims=True), snk)
                p = jnp.exp(s - mx)
                den = jnp.sum(p, axis=-1, keepdims=True) + jnp.exp(snk - mx)
                o = jnp.dot(p.astype(BF16), v_half[g, par][keys], preferred_element_type=F32)
                o_stack = o if o_stack is None else o_stack + o
                rden.append(1.0 / den)
            for j, hp in enumerate(pairs):
                blk = slice(j * QB_ATTN, (j + 1) * QB_ATTN)
                y = o_stack[blk] * jnp.where(low_q, rden[0][blk], rden[1][blk])
                path = path_ref[rows, hp * LANES:(hp + 1) * LANES].astype(F32)
                o_ref[rows, hp * LANES:(hp + 1) * LANES] = (y * _silu(path)).astype(BF16)


ATTN_PAIRS = A_HEADS // A_KV_HEADS // 2


def attn_bias_tables():
    t = np.arange(QB_ATTN)[:, None]
    s = np.arange(2 * A_WINDOW)[None, :]
    q_chunk = t // CHUNK + A_WINDOW // CHUNK
    k_chunk = s // CHUNK
    valid = (k_chunk <= q_chunk) & (k_chunk >= q_chunk - A_WINDOW // CHUNK)
    dist = np.abs(t + A_WINDOW - s).astype(np.float32)
    slopes = np.asarray(ALIBI_SLOPES, np.float32)[:, None, None]
    general = np.where(valid[None], -slopes * dist[None], -np.inf)
    first = np.where((valid & (s >= A_WINDOW))[None], -slopes * dist[None], -np.inf)
    per_head = np.stack([first, general]).astype(np.float32)
    tab = per_head.reshape(2, A_KV_HEADS, ATTN_PAIRS, 2, QB_ATTN, 2 * A_WINDOW).transpose(0, 1, 3, 2, 4, 5)
    return jnp.asarray(tab.reshape(2, A_KV_HEADS, 2, ATTN_PAIRS * QB_ATTN, 2 * A_WINDOW))


def window_attention(qpkv, sinks, bsz, seq, tq=TQ_ATTN):
    t = bsz * seq
    assert seq % tq == 0
    nq = seq // tq
    kvw = 2 * A_KV_HEADS * A_HD
    kv_col = (A_HEADS * A_HD + BRANCH_W) // kvw
    snk = jnp.broadcast_to(sinks.astype(F32)[:, None], (A_HEADS, LANES))
    bias = attn_bias_tables()
    return pl.pallas_call(
        functools.partial(_attn_kernel, tq=tq),
        grid=(bsz, nq),
        in_specs=[
            pl.BlockSpec((tq, A_HEADS * A_HD), lambda b, i: (b * nq + i, 0)),
            pl.BlockSpec((tq, kvw), lambda b, i: (b * nq + i, kv_col)),
            pl.BlockSpec((A_WINDOW, kvw),
                         lambda b, i: (jnp.maximum((b * nq + i) * (tq // A_WINDOW) - 1, 0), kv_col)),
            pl.BlockSpec((tq, BRANCH_W), lambda b, i: (b * nq + i, 1)),
            pl.BlockSpec((A_HEADS, LANES), lambda b, i: (0, 0)),
            pl.BlockSpec(bias.shape, lambda b, i: (0,) * bias.ndim),
        ],
        out_specs=pl.BlockSpec((tq, BRANCH_W), lambda b, i: (b * nq + i, 0)),
        out_shape=jax.ShapeDtypeStruct((t, BRANCH_W), BF16),
        compiler_params=_cparams(("parallel", "parallel")),
        name="window_attn",
    )(qpkv, qpkv, qpkv, qpkv, snk, bias)


CV_SLAB = 256
CV_GROUPS = 7


def _conv_proj_kernel(x_ref, w_ref, scw_ref, cfw_ref, lng_ref, lnb_ref, zb_ref, zc_ref,
                      ub_scr, uc_scr, cv_scr, pc_scr, *, ts):
    i = pl.program_id(1)
    n_sub = BRANCH_W // LANES

    @pl.when(i == 0)
    def _():
        ub_scr[:, 0:HALO, :] = jnp.zeros((n_sub, HALO, LANES), F32)
        uc_scr[:, 0:HALO, :] = jnp.zeros((n_sub, HALO, LANES), F32)

    x = x_ref[...]
    for s4 in range(BRANCH_W // CV_SLAB):
        c0 = s4 * CV_GROUPS * CV_SLAB
        proj = jnp.dot(x, w_ref[:, c0:c0 + CV_GROUPS * CV_SLAB], preferred_element_type=F32)
        for half in range(CV_SLAB // LANES):
            s = s4 * (CV_SLAB // LANES) + half
            ln = slice(s * LANES, (s + 1) * LANES)

            def grp(g, half=half, proj=proj):
                return proj[:, g * CV_SLAB + half * LANES:g * CV_SLAB + (half + 1) * LANES]

            ub_scr[s, HALO:HALO + ts, :] = grp(2) * grp(0)
            uc_scr[s, HALO:HALO + ts, :] = grp(3) * _sigmoid(grp(4))
            acc = jnp.zeros((ts, LANES), F32)
            for j in range(SC_WIDTH):
                win = ub_scr[s, pl.ds(HALO - (SC_WIDTH - 1) + j, ts, stride=1), :]
                acc = acc + scw_ref[j:j + 1, ln] * win
            zb_ref[:, ln] = (grp(1) * acc * _silu(grp(5))).astype(BF16)
            acc = jnp.zeros((ts, LANES), F32)
            for j in range(CF_WIDTH):
                win = uc_scr[s, pl.ds(HALO - (CF_WIDTH - 1) + j, ts, stride=1), :]
                acc = acc + cfw_ref[j:j + 1, ln] * win
            cv_scr[:, ln] = acc
            pc_scr[:, ln] = _silu(grp(6))
            ub_scr[s, 0:HALO, :] = ub_scr[s, ts:ts + HALO, :]
            uc_scr[s, 0:HALO, :] = uc_scr[s, ts:ts + HALO, :]

    yc = _silu(_ln_rows(cv_scr[...], lng_ref[...], lnb_ref[...]))
    zc_ref[...] = (yc * pc_scr[...]).astype(BF16)


def conv_mixers_proj(xb, w_cv, layer, sc_w, cf_w, cf_g, cf_b, bsz, seq, ts=TS_CONV):
    t, d = xb.shape
    assert seq % ts == 0
    ns = seq // ts
    w = BRANCH_W
    n_sub = w // LANES

    def full(shape):
        return pl.BlockSpec(shape, lambda b, i: (0, 0))

    out_spec = pl.BlockSpec((ts, w), lambda b, i: (b * ns + i, 0))
    return pl.pallas_call(
        functools.partial(_conv_proj_kernel, ts=ts),
        grid=(bsz, ns),
        in_specs=[pl.BlockSpec((ts, d), lambda b, i: (b * ns + i, 0)),
                  pl.BlockSpec((None,) + w_cv.shape[1:], lambda b, i: (layer, 0, 0),
                               pipeline_mode=pl.Buffered(1)),
                  full((SC_WIDTH, w)), full((CF_WIDTH, w)), full((1, w)), full((1, w))],
        out_specs=[out_spec, out_spec],
        out_shape=[jax.ShapeDtypeStruct((t, w), BF16), jax.ShapeDtypeStruct((t, w), BF16)],
        scratch_shapes=[pltpu.VMEM((n_sub, HALO + ts, LANES), F32),
                        pltpu.VMEM((n_sub, HALO + ts, LANES), F32),
                        pltpu.VMEM((ts, w), F32), pltpu.VMEM((ts, w), F32)],
        compiler_params=_cparams(("parallel", "arbitrary")),
        name="conv_mixers_proj",
    )(xb, w_cv, sc_w.astype(F32), cf_w.astype(F32), cf_g.reshape(1, w).astype(F32), cf_b.reshape(1, w).astype(F32))


def _log_sigmoid(x):
    return jnp.minimum(x, 0.0) - jnp.log(1.0 + jnp.exp(-jnp.abs(x)))


ML_HEAD_COLS = 2 * ML_DK + 3 * ML_DV


def _mlstm_proj_kernel(x_ref, w_ref, bias_ref, z_ref, c_scr, m_scr, *, tt, lc):
    c = pl.program_id(1)

    @pl.when(c == 0)
    def _():
        c_scr[...] = jnp.zeros_like(c_scr)
        m_scr[...] = jnp.zeros_like(m_scr)

    x = x_ref[...]
    gates_all = jnp.dot(x, w_ref[:, 0:LANES], preferred_element_type=F32) + bias_ref[...]
    row = lax.broadcasted_iota(jnp.int32, (lc, lc), 0)
    col = lax.broadcasted_iota(jnp.int32, (lc, lc), 1)
    causal = col <= row
    tri = causal.astype(F32)
    n_chunks = tt // lc
    per_chunk = []
    for ci in range(n_chunks):
        gates = gates_all[ci * lc:(ci + 1) * lc]
        bcum = jnp.dot(tri, _log_sigmoid(gates), preferred_element_type=F32,
                       precision=lax.Precision.HIGHEST)
        per_chunk.append((gates, gates.T, bcum, bcum.T))
    ones = jnp.ones((lc, LANES), BF16)
    scale = ML_DK ** -0.5
    for h in range(ML_HEADS):
        c0 = LANES + h * ML_HEAD_COLS
        proj = jnp.dot(x, w_ref[:, c0:c0 + ML_HEAD_COLS], preferred_element_type=F32)
        fl = ML_HEADS + h
        for ci in range(n_chunks):
            rows = slice(ci * lc, (ci + 1) * lc)
            gates, gates_t, bcum, bcum_t = per_chunk[ci]
            b_col = bcum[:, fl:fl + 1]
            b_row = bcum_t[fl:fl + 1, :]
            i_col = gates[:, h:h + 1]
            i_row = gates_t[h:h + 1, :]
            m_prev = m_scr[h:h + 1, 0:1]
            dmat = jnp.where(causal, b_col - b_row + i_row, -jnp.inf)
            inter = b_col + m_prev
            m_t = jnp.maximum(inter, jnp.max(dmat, axis=-1, keepdims=True))
            w_intra = jnp.exp(dmat - m_t)
            w_inter = jnp.exp(inter - m_t)
            qh = proj[rows, 0:ML_DK].astype(BF16)
            kh = proj[rows, ML_DK:2 * ML_DK]
            vh = proj[rows, 2 * ML_DK:2 * ML_DK + ML_DV].astype(BF16)
            og = proj[rows, 2 * ML_DK + ML_DV:2 * ML_DK + 2 * ML_DV]
            path = proj[rows, 2 * ML_DK + 2 * ML_DV:ML_HEAD_COLS]
            v_ext = jnp.concatenate([vh, ones], axis=1)
            qk = lax.dot_general(qh, kh.astype(BF16), (((1,), (1,)), ((), ())), preferred_element_type=F32)
            qk = qk * scale * w_intra
            state = c_scr[h]
            num_ext = (jnp.dot(qk.astype(BF16), v_ext, preferred_element_type=F32)
                       + w_inter * jnp.dot(qh, state.astype(BF16), preferred_element_type=F32))
            num = num_ext[:, :ML_DV]
            den = num_ext[:, ML_DV:]
            den = jnp.maximum(jnp.abs(den), jnp.exp(-m_t))
            hh = num / jnp.concatenate([den, den], axis=1)
            b_last = bcum[lc - 1:lc, fl:fl + 1]
            m_new = m_t[lc - 1:lc, :]
            w_state = jnp.exp(b_last - b_col + i_col - m_new)
            decay = jnp.exp(b_last + m_prev - m_new)
            kw_t = (kh * (scale * w_state)).T.astype(BF16)
            c_scr[h] = decay * state + jnp.dot(kw_t, v_ext, preferred_element_type=F32)
            m_scr[h:h + 1, :] = jnp.broadcast_to(m_new, (1, LANES))
            z_ref[rows, h * ML_DV:(h + 1) * ML_DV] = (_sigmoid(og) * hh * _silu(path)).astype(BF16)


def mlstm_proj(xb, w_ml, layer, bias_row, bsz, seq, tt=TT_MLSTM, lc=L_MLSTM):
    t, d = xb.shape
    assert seq % tt == 0 and tt % lc == 0
    nt = seq // tt
    return pl.pallas_call(
        functools.partial(_mlstm_proj_kernel, tt=tt, lc=lc),
        grid=(bsz, nt),
        in_specs=[
            pl.BlockSpec((tt, d), lambda b, c: (b * nt + c, 0)),
            pl.BlockSpec((None,) + w_ml.shape[1:], lambda b, c: (layer, 0, 0), pipeline_mode=pl.Buffered(1)),
            pl.BlockSpec((1, LANES), lambda b, c: (0, 0)),
        ],
        out_specs=pl.BlockSpec((tt, BRANCH_W), lambda b, c: (b * nt + c, 0)),
        out_shape=jax.ShapeDtypeStruct((t, BRANCH_W), BF16),
        scratch_shapes=[pltpu.VMEM((ML_HEADS, ML_DK, ML_DV + LANES), F32),
                        pltpu.VMEM((8, LANES), F32)],
        compiler_params=_cparams(("parallel", "arbitrary")),
        name="mlstm_proj",
    )(xb, w_ml, bias_row)


def _xattn_proj_kernel(x_ref, w_ref, kv_ref, o_ref):
    scale = XA_HD ** -0.5
    x = x_ref[...]
    for h in range(XA_HEADS):
        sl = slice(h * XA_HD, (h + 1) * XA_HD)
        proj = jnp.dot(x, w_ref[:, 2 * h * XA_HD:2 * (h + 1) * XA_HD], preferred_element_type=F32)
        qh = proj[:, :XA_HD].astype(BF16)
        path = proj[:, XA_HD:]
        kh = kv_ref[:, sl]
        vh = kv_ref[:, BRANCH_W + h * XA_HD:BRANCH_W + (h + 1) * XA_HD]
        s = lax.dot_general(qh, kh, (((1,), (1,)), ((), ())), preferred_element_type=F32) * scale
        mx = jnp.max(s, axis=-1, keepdims=True)
        p = jnp.exp(s - mx)
        den = jnp.sum(p, axis=-1, keepdims=True)
        o = jnp.dot(p.astype(BF16), vh, preferred_element_type=F32) / den
        o_ref[:, sl] = (o * _silu(path)).astype(BF16)


def cross_attention_proj(xb, w_xa, layer, kv, bsz, seq, mem_len, tq=TQ_XA):
    t, d = xb.shape
    assert seq % tq == 0
    nq = seq // tq
    return pl.pallas_call(
        _xattn_proj_kernel,
        grid=(bsz, nq),
        in_specs=[
            pl.BlockSpec((tq, d), lambda b, i: (b * nq + i, 0)),
            pl.BlockSpec((None,) + w_xa.shape[1:], lambda b, i: (layer, 0, 0), pipeline_mode=pl.Buffered(1)),
            pl.BlockSpec((mem_len, 2 * BRANCH_W), lambda b, i: (b, 0)),
        ],
        out_specs=pl.BlockSpec((tq, BRANCH_W), lambda b, i: (b * nq + i, 0)),
        out_shape=jax.ShapeDtypeStruct((t, BRANCH_W), BF16),
        compiler_params=_cparams(("parallel", "parallel")),
        name="cross_attn_proj",
    )(xb, w_xa, kv)


def _merge_kernel(x_ref, za_ref, zb_ref, zc_ref, zd_ref, ze_ref, wg_ref, wb_ref, o_ref):
    x = x_ref[...]
    acc = None
    for n, z_ref in enumerate((za_ref, zb_ref, zc_ref, zd_ref, ze_ref)):
        gate = _sigmoid(jnp.dot(x, wg_ref[n], preferred_element_type=F32))
        term = gate * jnp.dot(z_ref[...], wb_ref[n], preferred_element_type=F32)
        acc = term if acc is None else acc + term
    o_ref[...] = acc.astype(BF16)


def gated_merge(xb, zs, wg, wb, layer, tm=TM_MERGE, tn=TN_MERGE):
    t, d = xb.shape
    tm = min(tm, t)
    w = BRANCH_W
    z_spec = pl.BlockSpec((tm, w), lambda i, j: (i, 0))
    return pl.pallas_call(
        _merge_kernel,
        grid=(t // tm, d // tn),
        in_specs=[pl.BlockSpec((tm, d), lambda i, j: (i, 0)),
                  z_spec, z_spec, z_spec, z_spec, z_spec,
                  pl.BlockSpec((None, N_BRANCH, d, tn), lambda i, j: (layer, 0, 0, j)),
                  pl.BlockSpec((None, N_BRANCH, w, tn), lambda i, j: (layer, 0, 0, j))],
        out_specs=pl.BlockSpec((tm, tn), lambda i, j: (i, j)),
        out_shape=jax.ShapeDtypeStruct((t, d), BF16),
        compiler_params=_cparams(("parallel", "parallel")),
        name="gated_merge",
    )(xb, *zs, wg, wb)


def _out_kernel(m_ref, w_ref, x_ref, g_ref, b_ref, of_ref, ob_ref):
    out = jnp.dot(m_ref[...], w_ref[...], preferred_element_type=F32)
    y = _ln_rows(DN_ALPHA * x_ref[...] + out, g_ref[...], b_ref[...])
    of_ref[...] = y
    ob_ref[...] = y.astype(BF16)


def out_proj_ln(merged, w_out, layer, x, g, b, tm=TM_OUT):
    t, d = x.shape
    tm = min(tm, t)
    row = pl.BlockSpec((tm, d), lambda i: (i, 0))
    vec = pl.BlockSpec((1, d), lambda i: (0, 0))
    return pl.pallas_call(
        _out_kernel,
        grid=(t // tm,),
        in_specs=[row,
                  pl.BlockSpec((None, d, d), lambda i: (layer, 0, 0), pipeline_mode=pl.Buffered(1)),
                  row, vec, vec],
        out_specs=[row, row],
        out_shape=[jax.ShapeDtypeStruct((t, d), F32), jax.ShapeDtypeStruct((t, d), BF16)],
        compiler_params=_cparams(("parallel",)),
        name="out_proj_ln",
    )(merged, w_out, x, g.reshape(1, d), b.reshape(1, d))


def _regroup_segments():
    silu = lambda n: O_SILU + n * BRANCH_W
    attn = [(O_AQ, 0, BRANCH_W), (silu(0), BRANCH_W, BRANCH_W), (O_AK, 2 * BRANCH_W, O_SCH - O_AK)]
    conv = []
    for s in range(BRANCH_W // CV_SLAB):
        for gi, o in enumerate((O_SCH, O_SCBG, O_SCCG, O_CFV, O_CFG, silu(1), silu(2))):
            conv.append((o + s * CV_SLAB, (s * CV_GROUPS + gi) * CV_SLAB, CV_SLAB))
    ml = []
    for h in range(ML_HEADS):
        d0 = LANES + h * ML_HEAD_COLS
        ml += [(O_MLQ + h * ML_DK, d0, ML_DK), (O_MLK + h * ML_DK, d0 + ML_DK, ML_DK),
               (O_MLV + h * ML_DV, d0 + 2 * ML_DK, ML_DV), (O_MLO + h * ML_DV, d0 + 2 * ML_DK + ML_DV, ML_DV),
               (silu(3) + h * ML_DV, d0 + 2 * ML_DK + 2 * ML_DV, ML_DV)]
    xa = []
    for h in range(XA_HEADS):
        xa += [(O_XAQ + h * XA_HD, 2 * h * XA_HD, XA_HD), (silu(4) + h * XA_HD, (2 * h + 1) * XA_HD, XA_HD)]
    return attn, conv, ml, xa


def _regroup_kernel(wt_ref, attn_ref, conv_ref, ml_ref, xa_ref):
    for dst, segs in zip((attn_ref, conv_ref, ml_ref, xa_ref), _regroup_segments()):
        for src, d0, width in segs:
            dst[:, d0:d0 + width] = wt_ref[src:src + width, :].T.astype(BF16)
    tile = wt_ref[O_MLI:O_MLI + LANES, :].T
    lane = lax.broadcasted_iota(jnp.int32, tile.shape, 1)
    ml_ref[:, 0:LANES] = jnp.where(lane < 2 * ML_HEADS, tile, 0.0).astype(BF16)


def regroup_w_in(w_in, rb=128):
    depth, d, n_in = w_in.shape
    widths = (2 * BRANCH_W + O_SCH - O_AK, CV_GROUPS * BRANCH_W, LANES + ML_HEADS * ML_HEAD_COLS, 2 * BRANCH_W)
    return pl.pallas_call(
        _regroup_kernel,
        grid=(depth, d // rb),
        in_specs=[pl.BlockSpec((None, n_in, rb), lambda l, r: (l, 0, r))],
        out_specs=[pl.BlockSpec((None, rb, w), lambda l, r: (l, r, 0)) for w in widths],
        out_shape=[jax.ShapeDtypeStruct((depth, d, w), BF16) for w in widths],
        compiler_params=_cparams(("parallel", "parallel")),
        name="regroup_w_in",
    )(jnp.swapaxes(w_in, 1, 2))


def prepare_weights(w_in, w_gate, w_bout, w_out, mem_wkv, ml_i_bias, ml_f_bias):
    attn, conv, ml, xa = regroup_w_in(w_in)
    pad = LANES - 2 * ML_HEADS
    return dict(
        attn=attn, conv=conv, ml=ml, xa=xa,
        bias=jnp.pad(jnp.concatenate([ml_i_bias, ml_f_bias], axis=-1).astype(F32), ((0, 0), (0, pad))),
        kv=mem_wkv.astype(BF16),
        wg=w_gate.astype(BF16),
        wb=w_bout.astype(BF16),
        wo=w_out.astype(BF16),
    )


def hybrid_layer(xf, xb, mem_b, bsz, seq, wts, l, sinks, sc_w, cf_w, cf_g, cf_b, ln_g, ln_b):
    mem_len = mem_b.shape[0] // bsz
    qkvp = matmul(xb, wts["attn"], l, BF16, TM_MM, wts["attn"].shape[-1], "proj_attn")
    kv = matmul(mem_b, wts["kv"], l, BF16, TM_MM, 1024, "mem_kv")

    z_a = window_attention(qkvp, sinks, bsz, seq)
    z_b, z_c = conv_mixers_proj(xb, wts["conv"], l, sc_w, cf_w, cf_g, cf_b, bsz, seq)
    z_d = mlstm_proj(xb, wts["ml"], l, wts["bias"][l].reshape(1, LANES), bsz, seq)
    z_e = cross_attention_proj(xb, wts["xa"], l, kv, bsz, seq, mem_len)

    merged = gated_merge(xb, (z_a, z_b, z_c, z_d, z_e), wts["wg"], wts["wb"], l)
    return out_proj_ln(merged, wts["wo"], l, xf, ln_g, ln_b)


def kernel(x, mem, ln_in_g, ln_in_b, w_in, w_gate, w_bout, w_out, attn_sinks, sc_w, cf_w, cf_ln_g,
           cf_ln_b, ml_i_bias, ml_f_bias, mem_wkv, ln_g, ln_b):
    bsz, seq, d = x.shape
    wts = prepare_weights(w_in, w_gate, w_bout, w_out, mem_wkv, ml_i_bias, ml_f_bias)
    xf, xb = layer_norm_in(x.reshape(bsz * seq, d), ln_in_g, ln_in_b)
    mem_b = mem.reshape(-1, d).astype(BF16)
    for l in range(w_in.shape[0]):
        xf, xb = hybrid_layer(xf, xb, mem_b, bsz, seq, wts, l, attn_sinks[l], sc_w[l], cf_w[l],
                              cf_ln_g[l], cf_ln_b[l], ln_g[l], ln_b[l])
    return xf.reshape(bsz, seq, d)
```

```python
import functools

import jax
import jax.numpy as jnp
import numpy as np
from jax import lax
from jax.experimental import pallas as pl
from jax.experimental.pallas import tpu as pltpu

F32 = jnp.float32
BF16 = jnp.bfloat16

D_MODEL = 2048
DEPTH = 4
CHUNK = 64
BRANCH_W = 1024
N_BRANCH = 5
A_HEADS = 16
A_KV_HEADS = 2
A_HD = 64
A_WINDOW = 128
SC_WIDTH = 3
CF_WIDTH = 31
ML_HEADS = 4
ML_DK = 128
ML_DV = 256
XA_HEADS = 4
XA_HD = BRANCH_W // XA_HEADS
DN_ALPHA = (2 * DEPTH) ** 0.25
LN_EPS = 1e-5

IN_SPLITS = (
    A_HEADS * A_HD, A_KV_HEADS * A_HD, A_KV_HEADS * A_HD,
    BRANCH_W, BRANCH_W, BRANCH_W,
    BRANCH_W, BRANCH_W,
    ML_HEADS * ML_DK, ML_HEADS * ML_DK, ML_HEADS * ML_DV,
    ML_HEADS, ML_HEADS, BRANCH_W,
    BRANCH_W,
    N_BRANCH * BRANCH_W,
)
_OFF = tuple(int(v) for v in np.cumsum((0,) + IN_SPLITS))
(O_AQ, O_AK, O_AV, O_SCH, O_SCBG, O_SCCG, O_CFV, O_CFG, O_MLQ, O_MLK, O_MLV,
 O_MLI, O_MLF, O_MLO, O_XAQ, O_SILU, O_END) = _OFF

LANES = 128
HALO = 32
ALIBI_SLOPES = tuple(float(2.0 ** (-8.0 * (h + 1) / A_HEADS)) for h in range(A_HEADS))

TM_MM = 1024
TQ_ATTN = 2048
TS_CONV = 256
L_MLSTM = 256
TT_MLSTM = 256
TQ_XA = 512
TM_MERGE = 1024
TN_MERGE = 256
TM_OUT = 512
RC_OUT = 256
VMEM_LIMIT = 56 * 1024 * 1024


def _cparams(sem):
    return pltpu.CompilerParams(dimension_semantics=sem, vmem_limit_bytes=VMEM_LIMIT)


def _sigmoid(x):
    return 1.0 / (1.0 + jnp.exp(-x))


def _silu(x):
    return x * _sigmoid(x)


def _ln_rows(x, g, b):
    mu = jnp.mean(x, axis=-1, keepdims=True)
    xc = x - mu
    var = jnp.mean(xc * xc, axis=-1, keepdims=True)
    return xc * lax.rsqrt(var + LN_EPS) * g + b


def _ln_kernel(x_ref, g_ref, b_ref, of_ref, ob_ref):
    y = _ln_rows(x_ref[...], g_ref[...], b_ref[...])
    of_ref[...] = y
    ob_ref[...] = y.astype(BF16)


def layer_norm_in(x2d, g, b, tm=512):
    t, d = x2d.shape
    return pl.pallas_call(
        _ln_kernel,
        grid=(t // tm,),
        in_specs=[pl.BlockSpec((tm, d), lambda i: (i, 0)),
                  pl.BlockSpec((1, d), lambda i: (0, 0)),
                  pl.BlockSpec((1, d), lambda i: (0, 0))],
        out_specs=[pl.BlockSpec((tm, d), lambda i: (i, 0)),
                   pl.BlockSpec((tm, d), lambda i: (i, 0))],
        out_shape=[jax.ShapeDtypeStruct((t, d), F32), jax.ShapeDtypeStruct((t, d), BF16)],
        compiler_params=_cparams(("parallel",)),
        name="ln_in",
    )(x2d, g.reshape(1, d), b.reshape(1, d))


def _mm_kernel(a_ref, b_ref, o_ref):
    o_ref[...] = jnp.dot(a_ref[...], b_ref[...], preferred_element_type=F32).astype(o_ref.dtype)


def matmul(a, b, layer, out_dtype, tm, tn, name):
    m, k = a.shape
    n = b.shape[2]
    tm = min(tm, m)
    return pl.pallas_call(
        _mm_kernel,
        grid=(m // tm, n // tn),
        in_specs=[pl.BlockSpec((tm, k), lambda i, j: (i, 0)),
                  pl.BlockSpec((None, k, tn), lambda i, j: (layer, 0, j))],
        out_specs=pl.BlockSpec((tm, tn), lambda i, j: (i, j)),
        out_shape=jax.ShapeDtypeStruct((m, n), out_dtype),
        compiler_params=_cparams(("parallel", "parallel")),
        name=name,
    )(a, b)


QB_ATTN = A_WINDOW


def _attn_kernel(q_ref, kvc_ref, kvp_ref, path_ref, snk_ref, bias_ref, o_ref, *, tq):
    i = pl.program_id(1)
    nk = A_WINDOW + tq
    kv = jnp.concatenate([kvp_ref[...], kvc_ref[...]], axis=0)
    low = lax.broadcasted_iota(jnp.int32, (nk, LANES), 1) < A_HD
    low_q = lax.broadcasted_iota(jnp.int32, (QB_ATTN, LANES), 1) < A_HD
    zero = jnp.zeros((nk, LANES), BF16)

    def halves(blk):
        swapped = jnp.concatenate([blk[:, A_HD:], blk[:, :A_HD]], axis=1)
        return {(0, 0): jnp.where(low, blk, zero), (0, 1): jnp.where(low, zero, swapped),
                (1, 0): jnp.where(low, swapped, zero), (1, 1): jnp.where(low, zero, blk)}

    k_half = halves(kv[:, :LANES] * (A_HD ** -0.5))
    v_half = halves(kv[:, LANES:])
    for r in range(tq // QB_ATTN):
        rows = slice(r * QB_ATTN, (r + 1) * QB_ATTN)
        keys = slice(r * QB_ATTN, r * QB_ATTN + 2 * A_WINDOW)
        tbl = jnp.where(i == 0, 0, 1) if r == 0 else 1
        for g in range(A_KV_HEADS):
            pairs = range(g * ATTN_PAIRS, (g + 1) * ATTN_PAIRS)
            q_stack = jnp.concatenate([q_ref[rows, hp * LANES:(hp + 1) * LANES] for hp in pairs], axis=0)
            o_stack = None
            rden = []
            for par in range(2):
                s = lax.dot_general(q_stack, k_half[g, par][keys], (((1,), (1,)), ((), ())),
                                    preferred_element_type=F32)
                s = s + bias_ref[tbl, g, par]
                snk = jnp.concatenate(
                    [jnp.broadcast_to(snk_ref[2 * hp + par:2 * hp + par + 1, 0:1], (QB_ATTN, 1)) for hp in pairs],
                    axis=0)
                mx = jnp.maximum(jnp.max(s, axis=-1, keepdims=True), snk)
                p = jnp.exp(s - mx)
                den = jnp.sum(p, axis=-1, keepdims=True) + jnp.exp(snk - mx)
                o = jnp.dot(p.astype(BF16), v_half[g, par][keys], preferred_element_type=F32)
                o_stack = o if o_stack is None else o_stack + o
                rden.append(1.0 / den)
            for j, hp in enumerate(pairs):
                blk = slice(j * QB_ATTN, (j + 1) * QB_ATTN)
                y = o_stack[blk] * jnp.where(low_q, rden[0][blk], rden[1][blk])
                path = path_ref[rows, hp * LANES:(hp + 1) * LANES].astype(F32)
                o_ref[rows, hp * LANES:(hp + 1) * LANES] = (y * _silu(path)).astype(BF16)


ATTN_PAIRS = A_HEADS // A_KV_HEADS // 2


def attn_bias_tables():
    t = np.arange(QB_ATTN)[:, None]
    s = np.arange(2 * A_WINDOW)[None, :]
    q_chunk = t // CHUNK + A_WINDOW // CHUNK
    k_chunk = s // CHUNK
    valid = (k_chunk <= q_chunk) & (k_chunk >= q_chunk - A_WINDOW // CHUNK)
    dist = np.abs(t + A_WINDOW - s).astype(np.float32)
    slopes = np.asarray(ALIBI_SLOPES, np.float32)[:, None, None]
    general = np.where(valid[None], -slopes * dist[None], -np.inf)
    first = np.where((valid & (s >= A_WINDOW))[None], -slopes * dist[None], -np.inf)
    per_head = np.stack([first, general]).astype(np.float32)
    tab = per_head.reshape(2, A_KV_HEADS, ATTN_PAIRS, 2, QB_ATTN, 2 * A_WINDOW).transpose(0, 1, 3, 2, 4, 5)
    return jnp.asarray(tab.reshape(2, A_KV_HEADS, 2, ATTN_PAIRS * QB_ATTN, 2 * A_WINDOW))


def window_attention(qpkv, sinks, bsz, seq, tq=TQ_ATTN):
    t = bsz * seq
    assert seq % tq == 0
    nq = seq // tq
    kvw = 2 * A_KV_HEADS * A_HD
    kv_col = (A_HEADS * A_HD + BRANCH_W) // kvw
    snk = jnp.broadcast_to(sinks.astype(F32)[:, None], (A_HEADS, LANES))
    bias = attn_bias_tables()
    return pl.pallas_call(
        functools.partial(_attn_kernel, tq=tq),
        grid=(bsz, nq),
        in_specs=[
            pl.BlockSpec((tq, A_HEADS * A_HD), lambda b, i: (b * nq + i, 0)),
            pl.BlockSpec((tq, kvw), lambda b, i: (b * nq + i, kv_col)),
            pl.BlockSpec((A_WINDOW, kvw),
                         lambda b, i: (jnp.maximum((b * nq + i) * (tq // A_WINDOW) - 1, 0), kv_col)),
            pl.BlockSpec((tq, BRANCH_W), lambda b, i: (b * nq + i, 1)),
            pl.BlockSpec((A_HEADS, LANES), lambda b, i: (0, 0)),
            pl.BlockSpec(bias.shape, lambda b, i: (0,) * bias.ndim),
        ],
        out_specs=pl.BlockSpec((tq, BRANCH_W), lambda b, i: (b * nq + i, 0)),
        out_shape=jax.ShapeDtypeStruct((t, BRANCH_W), BF16),
        compiler_params=_cparams(("parallel", "parallel")),
        name="window_attn",
    )(qpkv, qpkv, qpkv, qpkv, snk, bias)


CV_SLAB = 256
CV_GROUPS = 7


def _conv_proj_kernel(x_ref, w_ref, scw_ref, cfw_ref, lng_ref, lnb_ref, zb_ref, zc_ref,
                      ub_scr, uc_scr, cv_scr, pc_scr, *, ts):
    i = pl.program_id(1)
    n_sub = BRANCH_W // LANES

    @pl.when(i == 0)
    def _():
        ub_scr[:, 0:HALO, :] = jnp.zeros((n_sub, HALO, LANES), F32)
        uc_scr[:, 0:HALO, :] = jnp.zeros((n_sub, HALO, LANES), F32)

    x = x_ref[...]
    for s4 in range(BRANCH_W // CV_SLAB):
        c0 = s4 * CV_GROUPS * CV_SLAB
        proj = jnp.dot(x, w_ref[:, c0:c0 + CV_GROUPS * CV_SLAB], preferred_element_type=F32)
        for half in range(CV_SLAB // LANES):
            s = s4 * (CV_SLAB // LANES) + half
            ln = slice(s * LANES, (s + 1) * LANES)

            def grp(g, half=half, proj=proj):
                return proj[:, g * CV_SLAB + half * LANES:g * CV_SLAB + (half + 1) * LANES]

            ub_scr[s, HALO:HALO + ts, :] = grp(2) * grp(0)
            uc_scr[s, HALO:HALO + ts, :] = grp(3) * _sigmoid(grp(4))
            acc = jnp.zeros((ts, LANES), F32)
            for j in range(SC_WIDTH):
                win = ub_scr[s, pl.ds(HALO - (SC_WIDTH - 1) + j, ts, stride=1), :]
                acc = acc + scw_ref[j:j + 1, ln] * win
            zb_ref[:, ln] = (grp(1) * acc * _silu(grp(5))).astype(BF16)
            acc = jnp.zeros((ts, LANES), F32)
            for j in range(CF_WIDTH):
                win = uc_scr[s, pl.ds(HALO - (CF_WIDTH - 1) + j, ts, stride=1), :]
                acc = acc + cfw_ref[j:j + 1, ln] * win
            cv_scr[:, ln] = acc
            pc_scr[:, ln] = _silu(grp(6))
            ub_scr[s, 0:HALO, :] = ub_scr[s, ts:ts + HALO, :]
            uc_scr[s, 0:HALO, :] = uc_scr[s, ts:ts + HALO, :]

    yc = _silu(_ln_rows(cv_scr[...], lng_ref[...], lnb_ref[...]))
    zc_ref[...] = (yc * pc_scr[...]).astype(BF16)


def conv_mixers_proj(xb, w_cv, layer, sc_w, cf_w, cf_g, cf_b, bsz, seq, ts=TS_CONV):
    t, d = xb.shape
    assert seq % ts == 0
    ns = seq // ts
    w = BRANCH_W
    n_sub = w // LANES

    def full(shape):
        return pl.BlockSpec(shape, lambda b, i: (0, 0))

    out_spec = pl.BlockSpec((ts, w), lambda b, i: (b * ns + i, 0))
    return pl.pallas_call(
        functools.partial(_conv_proj_kernel, ts=ts),
        grid=(bsz, ns),
        in_specs=[pl.BlockSpec((ts, d), lambda b, i: (b * ns + i, 0)),
                  pl.BlockSpec((None,) + w_cv.shape[1:], lambda b, i: (layer, 0, 0),
                               pipeline_mode=pl.Buffered(1)),
                  full((SC_WIDTH, w)), full((CF_WIDTH, w)), full((1, w)), full((1, w))],
        out_specs=[out_spec, out_spec],
        out_shape=[jax.ShapeDtypeStruct((t, w), BF16), jax.ShapeDtypeStruct((t, w), BF16)],
        scratch_shapes=[pltpu.VMEM((n_sub, HALO + ts, LANES), F32),
                        pltpu.VMEM((n_sub, HALO + ts, LANES), F32),
                        pltpu.VMEM((ts, w), F32), pltpu.VMEM((ts, w), F32)],
        compiler_params=_cparams(("parallel", "arbitrary")),
        name="conv_mixers_proj",
    )(xb, w_cv, sc_w.astype(F32), cf_w.astype(F32), cf_g.reshape(1, w).astype(F32), cf_b.reshape(1, w).astype(F32))


def _log_sigmoid(x):
    return jnp.minimum(x, 0.0) - jnp.log(1.0 + jnp.exp(-jnp.abs(x)))


ML_HEAD_COLS = 2 * ML_DK + 3 * ML_DV


def _mlstm_proj_kernel(x_ref, w_ref, bias_ref, z_ref, c_scr, m_scr, *, tt, lc):
    c = pl.program_id(1)

    @pl.when(c == 0)
    def _():
        c_scr[...] = jnp.zeros_like(c_scr)
        m_scr[...] = jnp.zeros_like(m_scr)

    x = x_ref[...]
    gates_all = jnp.dot(x, w_ref[:, 0:LANES], preferred_element_type=F32) + bias_ref[...]
    row = lax.broadcasted_iota(jnp.int32, (lc, lc), 0)
    col = lax.broadcasted_iota(jnp.int32, (lc, lc), 1)
    causal = col <= row
    tri = causal.astype(F32)
    n_chunks = tt // lc
    per_chunk = []
    for ci in range(n_chunks):
        gates = gates_all[ci * lc:(ci + 1) * lc]
        bcum = jnp.dot(tri, _log_sigmoid(gates), preferred_element_type=F32,
                       precision=lax.Precision.HIGHEST)
        per_chunk.append((gates, gates.T, bcum, bcum.T))
    ones = jnp.ones((lc, LANES), BF16)
    scale = ML_DK ** -0.5
    for h in range(ML_HEADS):
        c0 = LANES + h * ML_HEAD_COLS
        proj = jnp.dot(x, w_ref[:, c0:c0 + ML_HEAD_COLS], preferred_element_type=F32)
        fl = ML_HEADS + h
        for ci in range(n_chunks):
            rows = slice(ci * lc, (ci + 1) * lc)
            gates, gates_t, bcum, bcum_t = per_chunk[ci]
            b_col = bcum[:, fl:fl + 1]
            b_row = bcum_t[fl:fl + 1, :]
            i_col = gates[:, h:h + 1]
            i_row = gates_t[h:h + 1, :]
            m_prev = m_scr[h:h + 1, 0:1]
            dmat = jnp.where(causal, b_col - b_row + i_row, -jnp.inf)
            inter = b_col + m_prev
            m_t = jnp.maximum(inter, jnp.max(dmat, axis=-1, keepdims=True))
            w_intra = jnp.exp(dmat - m_t)
            w_inter = jnp.exp(inter - m_t)
            qh = proj[rows, 0:ML_DK].astype(BF16)
            kh = proj[rows, ML_DK:2 * ML_DK]
            vh = proj[rows, 2 * ML_DK:2 * ML_DK + ML_DV].astype(BF16)
            og = proj[rows, 2 * ML_DK + ML_DV:2 * ML_DK + 2 * ML_DV]
            path = proj[rows, 2 * ML_DK + 2 * ML_DV:ML_HEAD_COLS]
            v_ext = jnp.concatenate([vh, ones], axis=1)
            qk = lax.dot_general(qh, kh.astype(BF16), (((1,), (1,)), ((), ())), preferred_element_type=F32)
            qk = qk * scale * w_intra
            state = c_scr[h]
            num_ext = (jnp.dot(qk.astype(BF16), v_ext, preferred_element_type=F32)
                       + w_inter * jnp.dot(qh, state.astype(BF16), preferred_element_type=F32))
            num = num_ext[:, :ML_DV]
            den = num_ext[:, ML_DV:]
            den = jnp.maximum(jnp.abs(den), jnp.exp(-m_t))
            hh = num / jnp.concatenate([den, den], axis=1)
            b_last = bcum[lc - 1:lc, fl:fl + 1]
            m_new = m_t[lc - 1:lc, :]
            w_state = jnp.exp(b_last - b_col + i_col - m_new)
            decay = jnp.exp(b_last + m_prev - m_new)
            kw_t = (kh * (scale * w_state)).T.astype(BF16)
            c_scr[h] = decay * state + jnp.dot(kw_t, v_ext, preferred_element_type=F32)
            m_scr[h:h + 1, :] = jnp.broadcast_to(m_new, (1, LANES))
            z_ref[rows, h * ML_DV:(h + 1) * ML_DV] = (_sigmoid(og) * hh * _silu(path)).astype(BF16)


def mlstm_proj(xb, w_ml, layer, bias_row, bsz, seq, tt=TT_MLSTM, lc=L_MLSTM):
    t, d = xb.shape
    assert seq % tt == 0 and tt % lc == 0
    nt = seq // tt
    return pl.pallas_call(
        functools.partial(_mlstm_proj_kernel, tt=tt, lc=lc),
        grid=(bsz, nt),
        in_specs=[
            pl.BlockSpec((tt, d), lambda b, c: (b * nt + c, 0)),
            pl.BlockSpec((None,) + w_ml.shape[1:], lambda b, c: (layer, 0, 0), pipeline_mode=pl.Buffered(1)),
            pl.BlockSpec((1, LANES), lambda b, c: (0, 0)),
        ],
        out_specs=pl.BlockSpec((tt, BRANCH_W), lambda b, c: (b * nt + c, 0)),
        out_shape=jax.ShapeDtypeStruct((t, BRANCH_W), BF16),
        scratch_shapes=[pltpu.VMEM((ML_HEADS, ML_DK, ML_DV + LANES), F32),
                        pltpu.VMEM((8, LANES), F32)],
        compiler_params=_cparams(("parallel", "arbitrary")),
        name="mlstm_proj",
    )(xb, w_ml, bias_row)


def _xattn_proj_kernel(x_ref, w_ref, kv_ref, o_ref):
    scale = XA_HD ** -0.5
    x = x_ref[...]
    for h in range(XA_HEADS):
        sl = slice(h * XA_HD, (h + 1) * XA_HD)
        proj = jnp.dot(x, w_ref[:, 2 * h * XA_HD:2 * (h + 1) * XA_HD], preferred_element_type=F32)
        qh = proj[:, :XA_HD].astype(BF16)
        path = proj[:, XA_HD:]
        kh = kv_ref[:, sl]
        vh = kv_ref[:, BRANCH_W + h * XA_HD:BRANCH_W + (h + 1) * XA_HD]
        s = lax.dot_general(qh, kh, (((1,), (1,)), ((), ())), preferred_element_type=F32) * scale
        mx = jnp.max(s, axis=-1, keepdims=True)
        p = jnp.exp(s - mx)
        den = jnp.sum(p, axis=-1, keepdims=True)
        o = jnp.dot(p.astype(BF16), vh, preferred_element_type=F32) / den
        o_ref[:, sl] = (o * _silu(path)).astype(BF16)


def cross_attention_proj(xb, w_xa, layer, kv, bsz, seq, mem_len, tq=TQ_XA):
    t, d = xb.shape
    assert seq % tq == 0
    nq = seq // tq
    return pl.pallas_call(
        _xattn_proj_kernel,
        grid=(bsz, nq),
        in_specs=[
            pl.BlockSpec((tq, d), lambda b, i: (b * nq + i, 0)),
            pl.BlockSpec((None,) + w_xa.shape[1:], lambda b, i: (layer, 0, 0), pipeline_mode=pl.Buffered(1)),
            pl.BlockSpec((mem_len, 2 * BRANCH_W), lambda b, i: (b, 0)),
        ],
        out_specs=pl.BlockSpec((tq, BRANCH_W), lambda b, i: (b * nq + i, 0)),
        out_shape=jax.ShapeDtypeStruct((t, BRANCH_W), BF16),
        compiler_params=_cparams(("parallel", "parallel")),
        name="cross_attn_proj",
    )(xb, w_xa, kv)


def _merge_kernel(x_ref, za_ref, zb_ref, zc_ref, zd_ref, ze_ref, wg_ref, wb_ref, o_ref):
    x = x_ref[...]
    acc = None
    for n, z_ref in enumerate((za_ref, zb_ref, zc_ref, zd_ref, ze_ref)):
        gate = _sigmoid(jnp.dot(x, wg_ref[n], preferred_element_type=F32))
        term = gate * jnp.dot(z_ref[...], wb_ref[n], preferred_element_type=F32)
        acc = term if acc is None else acc + term
    o_ref[...] = acc.astype(BF16)


def gated_merge(xb, zs, wg, wb, layer, tm=TM_MERGE, tn=TN_MERGE):
    t, d = xb.shape
    tm = min(tm, t)
    w = BRANCH_W
    z_spec = pl.BlockSpec((tm, w), lambda i, j: (i, 0))
    return pl.pallas_call(
        _merge_kernel,
        grid=(t // tm, d // tn),
        in_specs=[pl.BlockSpec((tm, d), lambda i, j: (i, 0)),
                  z_spec, z_spec, z_spec, z_spec, z_spec,
                  pl.BlockSpec((None, N_BRANCH, d, tn), lambda i, j: (layer, 0, 0, j)),
                  pl.BlockSpec((None, N_BRANCH, w, tn), lambda i, j: (layer, 0, 0, j))],
        out_specs=pl.BlockSpec((tm, tn), lambda i, j: (i, j)),
        out_shape=jax.ShapeDtypeStruct((t, d), BF16),
        compiler_params=_cparams(("parallel", "parallel")),
        name="gated_merge",
    )(xb, *zs, wg, wb)


def _out_kernel(m_ref, w_ref, x_ref, g_ref, b_ref, of_ref, ob_ref, *, rc):
    for r in range(m_ref.shape[0] // rc):
        rows = slice(r * rc, (r + 1) * rc)
        out = jnp.dot(m_ref[rows, :], w_ref[...], preferred_element_type=F32)
        y = _ln_rows(DN_ALPHA * x_ref[rows, :] + out, g_ref[...], b_ref[...])
        of_ref[rows, :] = y
        ob_ref[rows, :] = y.astype(BF16)


def out_proj_ln(merged, w_out, layer, x, g, b, tm=TM_OUT):
    t, d = x.shape
    tm = min(tm, t)
    row = pl.BlockSpec((tm, d), lambda i: (i, 0))
    vec = pl.BlockSpec((1, d), lambda i: (0, 0))
    return pl.pallas_call(
        functools.partial(_out_kernel, rc=min(RC_OUT, tm)),
        grid=(t // tm,),
        in_specs=[row,
                  pl.BlockSpec((None, d, d), lambda i: (layer, 0, 0), pipeline_mode=pl.Buffered(1)),
                  row, vec, vec],
        out_specs=[row, row],
        out_shape=[jax.ShapeDtypeStruct((t, d), F32), jax.ShapeDtypeStruct((t, d), BF16)],
        compiler_params=_cparams(("parallel",)),
        name="out_proj_ln",
    )(merged, w_out, x, g.reshape(1, d), b.reshape(1, d))


def _regroup_segments():
    silu = lambda n: O_SILU + n * BRANCH_W
    attn = [(O_AQ, 0, BRANCH_W), (silu(0), BRANCH_W, BRANCH_W), (O_AK, 2 * BRANCH_W, O_SCH - O_AK)]
    conv = []
    for s in range(BRANCH_W // CV_SLAB):
        for gi, o in enumerate((O_SCH, O_SCBG, O_SCCG, O_CFV, O_CFG, silu(1), silu(2))):
            conv.append((o + s * CV_SLAB, (s * CV_GROUPS + gi) * CV_SLAB, CV_SLAB))
    ml = []
    for h in range(ML_HEADS):
        d0 = LANES + h * ML_HEAD_COLS
        ml += [(O_MLQ + h * ML_DK, d0, ML_DK), (O_MLK + h * ML_DK, d0 + ML_DK, ML_DK),
               (O_MLV + h * ML_DV, d0 + 2 * ML_DK, ML_DV), (O_MLO + h * ML_DV, d0 + 2 * ML_DK + ML_DV, ML_DV),
               (silu(3) + h * ML_DV, d0 + 2 * ML_DK + 2 * ML_DV, ML_DV)]
    xa = []
    for h in range(XA_HEADS):
        xa += [(O_XAQ + h * XA_HD, 2 * h * XA_HD, XA_HD), (silu(4) + h * XA_HD, (2 * h + 1) * XA_HD, XA_HD)]
    return attn, conv, ml, xa


def _regroup_kernel(wt_ref, attn_ref, conv_ref, ml_ref, xa_ref):
    for dst, segs in zip((attn_ref, conv_ref, ml_ref, xa_ref), _regroup_segments()):
        for src, d0, width in segs:
            dst[:, d0:d0 + width] = wt_ref[src:src + width, :].T.astype(BF16)
    tile = wt_ref[O_MLI:O_MLI + LANES, :].T
    lane = lax.broadcasted_iota(jnp.int32, tile.shape, 1)
    ml_ref[:, 0:LANES] = jnp.where(lane < 2 * ML_HEADS, tile, 0.0).astype(BF16)


def regroup_w_in(w_in, rb=128):
    depth, d, n_in = w_in.shape
    widths = (2 * BRANCH_W + O_SCH - O_AK, CV_GROUPS * BRANCH_W, LANES + ML_HEADS * ML_HEAD_COLS, 2 * BRANCH_W)
    return pl.pallas_call(
        _regroup_kernel,
        grid=(depth, d // rb),
        in_specs=[pl.BlockSpec((None, n_in, rb), lambda l, r: (l, 0, r))],
        out_specs=[pl.BlockSpec((None, rb, w), lambda l, r: (l, r, 0)) for w in widths],
        out_shape=[jax.ShapeDtypeStruct((depth, d, w), BF16) for w in widths],
        compiler_params=_cparams(("parallel", "parallel")),
        name="regroup_w_in",
    )(jnp.swapaxes(w_in, 1, 2))


def prepare_weights(w_in, w_gate, w_bout, w_out, mem_wkv, ml_i_bias, ml_f_bias):
    attn, conv, ml, xa = regroup_w_in(w_in)
    pad = LANES - 2 * ML_HEADS
    return dict(
        attn=attn, conv=conv, ml=ml, xa=xa,
        bias=jnp.pad(jnp.concatenate([ml_i_bias, ml_f_bias], axis=-1).astype(F32), ((0, 0), (0, pad))),
        kv=mem_wkv.astype(BF16),
        wg=w_gate.astype(BF16),
        wb=w_bout.astype(BF16),
        wo=w_out.astype(BF16),
    )


def hybrid_layer(xf, xb, mem_b, bsz, seq, wts, l, sinks, sc_w, cf_w, cf_g, cf_b, ln_g, ln_b):
    mem_len = mem_b.shape[0] // bsz
    qkvp = matmul(xb, wts["attn"], l, BF16, TM_MM, wts["attn"].shape[-1], "proj_attn")
    kv = matmul(mem_b, wts["kv"], l, BF16, TM_MM, 1024, "mem_kv")

    z_a = window_attention(qkvp, sinks, bsz, seq)
    z_b, z_c = conv_mixers_proj(xb, wts["conv"], l, sc_w, cf_w, cf_g, cf_b, bsz, seq)
    z_d = mlstm_proj(xb, wts["ml"], l, wts["bias"][l].reshape(1, LANES), bsz, seq)
    z_e = cross_attention_proj(xb, wts["xa"], l, kv, bsz, seq, mem_len)

    merged = gated_merge(xb, (z_a, z_b, z_c, z_d, z_e), wts["wg"], wts["wb"], l)
    return out_proj_ln(merged, wts["wo"], l, xf, ln_g, ln_b)


def kernel(x, mem, ln_in_g, ln_in_b, w_in, w_gate, w_bout, w_out, attn_sinks, sc_w, cf_w, cf_ln_g,
           cf_ln_b, ml_i_bias, ml_f_bias, mem_wkv, ln_g, ln_b):
    bsz, seq, d = x.shape
    wts = prepare_weights(w_in, w_gate, w_bout, w_out, mem_wkv, ml_i_bias, ml_f_bias)
    xf, xb = layer_norm_in(x.reshape(bsz * seq, d), ln_in_g, ln_in_b)
    mem_b = mem.reshape(-1, d).astype(BF16)
    for l in range(w_in.shape[0]):
        xf, xb = hybrid_layer(xf, xb, mem_b, bsz, seq, wts, l, attn_sinks[l], sc_w[l], cf_w[l],
                              cf_ln_g[l], cf_ln_b[l], ln_g[l], ln_b[l])
    return xf.reshape(bsz, seq, d)
```

```python
import functools

import jax
import jax.numpy as jnp
import numpy as np
from jax import lax
from jax.experimental import pallas as pl
from jax.experimental.pallas import tpu as pltpu

F32 = jnp.float32
BF16 = jnp.bfloat16

D_MODEL = 2048
DEPTH = 4
CHUNK = 64
BRANCH_W = 1024
N_BRANCH = 5
A_HEADS = 16
A_KV_HEADS = 2
A_HD = 64
A_WINDOW = 128
SC_WIDTH = 3
CF_WIDTH = 31
ML_HEADS = 4
ML_DK = 128
ML_DV = 256
XA_HEADS = 4
XA_HD = BRANCH_W // XA_HEADS
DN_ALPHA = (2 * DEPTH) ** 0.25
LN_EPS = 1e-5

IN_SPLITS = (
    A_HEADS * A_HD, A_KV_HEADS * A_HD, A_KV_HEADS * A_HD,
    BRANCH_W, BRANCH_W, BRANCH_W,
    BRANCH_W, BRANCH_W,
    ML_HEADS * ML_DK, ML_HEADS * ML_DK, ML_HEADS * ML_DV,
    ML_HEADS, ML_HEADS, BRANCH_W,
    BRANCH_W,
    N_BRANCH * BRANCH_W,
)
_OFF = tuple(int(v) for v in np.cumsum((0,) + IN_SPLITS))
(O_AQ, O_AK, O_AV, O_SCH, O_SCBG, O_SCCG, O_CFV, O_CFG, O_MLQ, O_MLK, O_MLV,
 O_MLI, O_MLF, O_MLO, O_XAQ, O_SILU, O_END) = _OFF

LANES = 128
HALO = 32
ALIBI_SLOPES = tuple(float(2.0 ** (-8.0 * (h + 1) / A_HEADS)) for h in range(A_HEADS))

TM_MM = 1024
TQ_ATTN = 2048
TS_CONV = 256
L_MLSTM = 256
TT_MLSTM = 256
TQ_XA = 512
TM_MERGE = 1024
TN_MERGE = 256
TM_OUT = 512
RC_OUT = 256
VMEM_LIMIT = 56 * 1024 * 1024


def _cparams(sem):
    return pltpu.CompilerParams(dimension_semantics=sem, vmem_limit_bytes=VMEM_LIMIT)


def _sigmoid(x):
    return 1.0 / (1.0 + jnp.exp(-x))


def _silu(x):
    return x * _sigmoid(x)


def _ln_rows(x, g, b):
    mu = jnp.mean(x, axis=-1, keepdims=True)
    xc = x - mu
    var = jnp.mean(xc * xc, axis=-1, keepdims=True)
    return xc * lax.rsqrt(var + LN_EPS) * g + b


def _ln_kernel(x_ref, g_ref, b_ref, of_ref, ob_ref):
    y = _ln_rows(x_ref[...], g_ref[...], b_ref[...])
    of_ref[...] = y
    ob_ref[...] = y.astype(BF16)


def layer_norm_in(x2d, g, b, tm=512):
    t, d = x2d.shape
    return pl.pallas_call(
        _ln_kernel,
        grid=(t // tm,),
        in_specs=[pl.BlockSpec((tm, d), lambda i: (i, 0)),
                  pl.BlockSpec((1, d), lambda i: (0, 0)),
                  pl.BlockSpec((1, d), lambda i: (0, 0))],
        out_specs=[pl.BlockSpec((tm, d), lambda i: (i, 0)),
                   pl.BlockSpec((tm, d), lambda i: (i, 0))],
        out_shape=[jax.ShapeDtypeStruct((t, d), F32), jax.ShapeDtypeStruct((t, d), BF16)],
        compiler_params=_cparams(("parallel",)),
        name="ln_in",
    )(x2d, g.reshape(1, d), b.reshape(1, d))


def _mm_kernel(a_ref, b_ref, o_ref):
    o_ref[...] = jnp.dot(a_ref[...], b_ref[...].astype(BF16), preferred_element_type=F32).astype(o_ref.dtype)


def matmul(a, b, layer, out_dtype, tm, tn, name):
    m, k = a.shape
    n = b.shape[2]
    tm = min(tm, m)
    return pl.pallas_call(
        _mm_kernel,
        grid=(m // tm, n // tn),
        in_specs=[pl.BlockSpec((tm, k), lambda i, j: (i, 0)),
                  pl.BlockSpec((None, k, tn), lambda i, j: (layer, 0, j))],
        out_specs=pl.BlockSpec((tm, tn), lambda i, j: (i, j)),
        out_shape=jax.ShapeDtypeStruct((m, n), out_dtype),
        compiler_params=_cparams(("parallel", "parallel")),
        name=name,
    )(a, b)


QB_ATTN = A_WINDOW


def _attn_kernel(q_ref, kvc_ref, kvp_ref, path_ref, snk_ref, bias_ref, o_ref, *, tq):
    i = pl.program_id(1)
    nk = A_WINDOW + tq
    kv = jnp.concatenate([kvp_ref[...], kvc_ref[...]], axis=0)
    low = lax.broadcasted_iota(jnp.int32, (nk, LANES), 1) < A_HD
    low_q = lax.broadcasted_iota(jnp.int32, (QB_ATTN, LANES), 1) < A_HD
    zero = jnp.zeros((nk, LANES), BF16)

    def halves(blk):
        swapped = jnp.concatenate([blk[:, A_HD:], blk[:, :A_HD]], axis=1)
        return {(0, 0): jnp.where(low, blk, zero), (0, 1): jnp.where(low, zero, swapped),
                (1, 0): jnp.where(low, swapped, zero), (1, 1): jnp.where(low, zero, blk)}

    k_half = halves(kv[:, :LANES] * (A_HD ** -0.5))
    v_half = halves(kv[:, LANES:])
    for r in range(tq // QB_ATTN):
        rows = slice(r * QB_ATTN, (r + 1) * QB_ATTN)
        keys = slice(r * QB_ATTN, r * QB_ATTN + 2 * A_WINDOW)
        tbl = jnp.where(i == 0, 0, 1) if r == 0 else 1
        for g in range(A_KV_HEADS):
            pairs = range(g * ATTN_PAIRS, (g + 1) * ATTN_PAIRS)
            q_stack = jnp.concatenate([q_ref[rows, hp * LANES:(hp + 1) * LANES] for hp in pairs], axis=0)
            o_stack = None
            rden = []
            for par in range(2):
                s = lax.dot_general(q_stack, k_half[g, par][keys], (((1,), (1,)), ((), ())),
                                    preferred_element_type=F32)
                s = s + bias_ref[tbl, g, par]
                snk = jnp.concatenate(
                    [jnp.broadcast_to(snk_ref[2 * hp + par:2 * hp + par + 1, 0:1], (QB_ATTN, 1)) for hp in pairs],
                    axis=0)
                mx = jnp.maximum(jnp.max(s, axis=-1, keepdims=True), snk)
                p = jnp.exp(s - mx)
                den = jnp.sum(p, axis=-1, keepdims=True) + jnp.exp(snk - mx)
                o = jnp.dot(p.astype(BF16), v_half[g, par][keys], preferred_element_type=F32)
                o_stack = o if o_stack is None else o_stack + o
                rden.append(1.0 / den)
            for j, hp in enumerate(pairs):
                blk = slice(j * QB_ATTN, (j + 1) * QB_ATTN)
                y = o_stack[blk] * jnp.where(low_q, rden[0][blk], rden[1][blk])
                path = path_ref[rows, hp * LANES:(hp + 1) * LANES].astype(F32)
                o_ref[rows, hp * LANES:(hp + 1) * LANES] = (y * _silu(path)).astype(BF16)


ATTN_PAIRS = A_HEADS // A_KV_HEADS // 2


def attn_bias_tables():
    t = np.arange(QB_ATTN)[:, None]
    s = np.arange(2 * A_WINDOW)[None, :]
    q_chunk = t // CHUNK + A_WINDOW // CHUNK
    k_chunk = s // CHUNK
    valid = (k_chunk <= q_chunk) & (k_chunk >= q_chunk - A_WINDOW // CHUNK)
    dist = np.abs(t + A_WINDOW - s).astype(np.float32)
    slopes = np.asarray(ALIBI_SLOPES, np.float32)[:, None, None]
    general = np.where(valid[None], -slopes * dist[None], -np.inf)
    first = np.where((valid & (s >= A_WINDOW))[None], -slopes * dist[None], -np.inf)
    per_head = np.stack([first, general]).astype(np.float32)
    tab = per_head.reshape(2, A_KV_HEADS, ATTN_PAIRS, 2, QB_ATTN, 2 * A_WINDOW).transpose(0, 1, 3, 2, 4, 5)
    return jnp.asarray(tab.reshape(2, A_KV_HEADS, 2, ATTN_PAIRS * QB_ATTN, 2 * A_WINDOW))


def window_attention(qpkv, sinks, bsz, seq, tq=TQ_ATTN):
    t = bsz * seq
    assert seq % tq == 0
    nq = seq // tq
    kvw = 2 * A_KV_HEADS * A_HD
    kv_col = (A_HEADS * A_HD + BRANCH_W) // kvw
    snk = jnp.broadcast_to(sinks.astype(F32)[:, None], (A_HEADS, LANES))
    bias = attn_bias_tables()
    return pl.pallas_call(
        functools.partial(_attn_kernel, tq=tq),
        grid=(bsz, nq),
        in_specs=[
            pl.BlockSpec((tq, A_HEADS * A_HD), lambda b, i: (b * nq + i, 0)),
            pl.BlockSpec((tq, kvw), lambda b, i: (b * nq + i, kv_col)),
            pl.BlockSpec((A_WINDOW, kvw),
                         lambda b, i: (jnp.maximum((b * nq + i) * (tq // A_WINDOW) - 1, 0), kv_col)),
            pl.BlockSpec((tq, BRANCH_W), lambda b, i: (b * nq + i, 1)),
            pl.BlockSpec((A_HEADS, LANES), lambda b, i: (0, 0)),
            pl.BlockSpec(bias.shape, lambda b, i: (0,) * bias.ndim),
        ],
        out_specs=pl.BlockSpec((tq, BRANCH_W), lambda b, i: (b * nq + i, 0)),
        out_shape=jax.ShapeDtypeStruct((t, BRANCH_W), BF16),
        compiler_params=_cparams(("parallel", "parallel")),
        name="window_attn",
    )(qpkv, qpkv, qpkv, qpkv, snk, bias)


CV_SLAB = 256
CV_GROUPS = 7


def _conv_proj_kernel(x_ref, w_ref, scw_ref, cfw_ref, lng_ref, lnb_ref, zb_ref, zc_ref,
                      ub_scr, uc_scr, cv_scr, pc_scr, *, ts):
    i = pl.program_id(1)
    n_sub = BRANCH_W // LANES

    @pl.when(i == 0)
    def _():
        ub_scr[:, 0:HALO, :] = jnp.zeros((n_sub, HALO, LANES), F32)
        uc_scr[:, 0:HALO, :] = jnp.zeros((n_sub, HALO, LANES), F32)

    x = x_ref[...]
    for s4 in range(BRANCH_W // CV_SLAB):
        c0 = s4 * CV_GROUPS * CV_SLAB
        proj = jnp.dot(x, w_ref[:, c0:c0 + CV_GROUPS * CV_SLAB], preferred_element_type=F32)
        for half in range(CV_SLAB // LANES):
            s = s4 * (CV_SLAB // LANES) + half
            ln = slice(s * LANES, (s + 1) * LANES)

            def grp(g, half=half, proj=proj):
                return proj[:, g * CV_SLAB + half * LANES:g * CV_SLAB + (half + 1) * LANES]

            ub_scr[s, HALO:HALO + ts, :] = grp(2) * grp(0)
            uc_scr[s, HALO:HALO + ts, :] = grp(3) * _sigmoid(grp(4))
            acc = jnp.zeros((ts, LANES), F32)
            for j in range(SC_WIDTH):
                win = ub_scr[s, pl.ds(HALO - (SC_WIDTH - 1) + j, ts, stride=1), :]
                acc = acc + scw_ref[j:j + 1, ln] * win
            zb_ref[:, ln] = (grp(1) * acc * _silu(grp(5))).astype(BF16)
            acc = jnp.zeros((ts, LANES), F32)
            for j in range(CF_WIDTH):
                win = uc_scr[s, pl.ds(HALO - (CF_WIDTH - 1) + j, ts, stride=1), :]
                acc = acc + cfw_ref[j:j + 1, ln] * win
            cv_scr[:, ln] = acc
            pc_scr[:, ln] = _silu(grp(6))
            ub_scr[s, 0:HALO, :] = ub_scr[s, ts:ts + HALO, :]
            uc_scr[s, 0:HALO, :] = uc_scr[s, ts:ts + HALO, :]

    yc = _silu(_ln_rows(cv_scr[...], lng_ref[...], lnb_ref[...]))
    zc_ref[...] = (yc * pc_scr[...]).astype(BF16)


def conv_mixers_proj(xb, w_cv, layer, sc_w, cf_w, cf_g, cf_b, bsz, seq, ts=TS_CONV):
    t, d = xb.shape
    assert seq % ts == 0
    ns = seq // ts
    w = BRANCH_W
    n_sub = w // LANES

    def full(shape):
        return pl.BlockSpec(shape, lambda b, i: (0, 0))

    out_spec = pl.BlockSpec((ts, w), lambda b, i: (b * ns + i, 0))
    return pl.pallas_call(
        functools.partial(_conv_proj_kernel, ts=ts),
        grid=(bsz, ns),
        in_specs=[pl.BlockSpec((ts, d), lambda b, i: (b * ns + i, 0)),
                  pl.BlockSpec((None,) + w_cv.shape[1:], lambda b, i: (layer, 0, 0),
                               pipeline_mode=pl.Buffered(1)),
                  full((SC_WIDTH, w)), full((CF_WIDTH, w)), full((1, w)), full((1, w))],
        out_specs=[out_spec, out_spec],
        out_shape=[jax.ShapeDtypeStruct((t, w), BF16), jax.ShapeDtypeStruct((t, w), BF16)],
        scratch_shapes=[pltpu.VMEM((n_sub, HALO + ts, LANES), F32),
                        pltpu.VMEM((n_sub, HALO + ts, LANES), F32),
                        pltpu.VMEM((ts, w), F32), pltpu.VMEM((ts, w), F32)],
        compiler_params=_cparams(("parallel", "arbitrary")),
        name="conv_mixers_proj",
    )(xb, w_cv, sc_w.astype(F32), cf_w.astype(F32), cf_g.reshape(1, w).astype(F32), cf_b.reshape(1, w).astype(F32))


def _log_sigmoid(x):
    return jnp.minimum(x, 0.0) - jnp.log(1.0 + jnp.exp(-jnp.abs(x)))


ML_HEAD_COLS = 2 * ML_DK + 3 * ML_DV


def _mlstm_proj_kernel(x_ref, w_ref, bias_ref, z_ref, c_scr, m_scr, *, tt, lc):
    c = pl.program_id(1)

    @pl.when(c == 0)
    def _():
        c_scr[...] = jnp.zeros_like(c_scr)
        m_scr[...] = jnp.zeros_like(m_scr)

    x = x_ref[...]
    gates_all = jnp.dot(x, w_ref[:, 0:LANES], preferred_element_type=F32) + bias_ref[...]
    row = lax.broadcasted_iota(jnp.int32, (lc, lc), 0)
    col = lax.broadcasted_iota(jnp.int32, (lc, lc), 1)
    causal = col <= row
    tri = causal.astype(F32)
    n_chunks = tt // lc
    per_chunk = []
    for ci in range(n_chunks):
        gates = gates_all[ci * lc:(ci + 1) * lc]
        bcum = jnp.dot(tri, _log_sigmoid(gates), preferred_element_type=F32,
                       precision=lax.Precision.HIGHEST)
        per_chunk.append((gates, gates.T, bcum, bcum.T))
    ones = jnp.ones((lc, LANES), BF16)
    scale = ML_DK ** -0.5
    for h in range(ML_HEADS):
        c0 = LANES + h * ML_HEAD_COLS
        proj = jnp.dot(x, w_ref[:, c0:c0 + ML_HEAD_COLS], preferred_element_type=F32)
        fl = ML_HEADS + h
        for ci in range(n_chunks):
            rows = slice(ci * lc, (ci + 1) * lc)
            gates, gates_t, bcum, bcum_t = per_chunk[ci]
            b_col = bcum[:, fl:fl + 1]
            b_row = bcum_t[fl:fl + 1, :]
            i_col = gates[:, h:h + 1]
            i_row = gates_t[h:h + 1, :]
            m_prev = m_scr[h:h + 1, 0:1]
            dmat = jnp.where(causal, b_col - b_row + i_row, -jnp.inf)
            inter = b_col + m_prev
            m_t = jnp.maximum(inter, jnp.max(dmat, axis=-1, keepdims=True))
            w_intra = jnp.exp(dmat - m_t)
            w_inter = jnp.exp(inter - m_t)
            qh = proj[rows, 0:ML_DK].astype(BF16)
            kh = proj[rows, ML_DK:2 * ML_DK]
            vh = proj[rows, 2 * ML_DK:2 * ML_DK + ML_DV].astype(BF16)
            og = proj[rows, 2 * ML_DK + ML_DV:2 * ML_DK + 2 * ML_DV]
            path = proj[rows, 2 * ML_DK + 2 * ML_DV:ML_HEAD_COLS]
            v_ext = jnp.concatenate([vh, ones], axis=1)
            qk = lax.dot_general(qh, kh.astype(BF16), (((1,), (1,)), ((), ())), preferred_element_type=F32)
            qk = qk * scale * w_intra
            state = c_scr[h]
            num_ext = (jnp.dot(qk.astype(BF16), v_ext, preferred_element_type=F32)
                       + w_inter * jnp.dot(qh, state.astype(BF16), preferred_element_type=F32))
            num = num_ext[:, :ML_DV]
            den = num_ext[:, ML_DV:]
            den = jnp.maximum(jnp.abs(den), jnp.exp(-m_t))
            hh = num / jnp.concatenate([den, den], axis=1)
            b_last = bcum[lc - 1:lc, fl:fl + 1]
            m_new = m_t[lc - 1:lc, :]
            w_state = jnp.exp(b_last - b_col + i_col - m_new)
            decay = jnp.exp(b_last + m_prev - m_new)
            kw_t = (kh * (scale * w_state)).T.astype(BF16)
            c_scr[h] = decay * state + jnp.dot(kw_t, v_ext, preferred_element_type=F32)
            m_scr[h:h + 1, :] = jnp.broadcast_to(m_new, (1, LANES))
            z_ref[rows, h * ML_DV:(h + 1) * ML_DV] = (_sigmoid(og) * hh * _silu(path)).astype(BF16)


def mlstm_proj(xb, w_ml, layer, bias_row, bsz, seq, tt=TT_MLSTM, lc=L_MLSTM):
    t, d = xb.shape
    assert seq % tt == 0 and tt % lc == 0
    nt = seq // tt
    return pl.pallas_call(
        functools.partial(_mlstm_proj_kernel, tt=tt, lc=lc),
        grid=(bsz, nt),
        in_specs=[
            pl.BlockSpec((tt, d), lambda b, c: (b * nt + c, 0)),
            pl.BlockSpec((None,) + w_ml.shape[1:], lambda b, c: (layer, 0, 0), pipeline_mode=pl.Buffered(1)),
            pl.BlockSpec((1, LANES), lambda b, c: (0, 0)),
        ],
        out_specs=pl.BlockSpec((tt, BRANCH_W), lambda b, c: (b * nt + c, 0)),
        out_shape=jax.ShapeDtypeStruct((t, BRANCH_W), BF16),
        scratch_shapes=[pltpu.VMEM((ML_HEADS, ML_DK, ML_DV + LANES), F32),
                        pltpu.VMEM((8, LANES), F32)],
        compiler_params=_cparams(("parallel", "arbitrary")),
        name="mlstm_proj",
    )(xb, w_ml, bias_row)


def _xattn_proj_kernel(x_ref, w_ref, kv_ref, o_ref):
    scale = XA_HD ** -0.5
    x = x_ref[...]
    for h in range(XA_HEADS):
        sl = slice(h * XA_HD, (h + 1) * XA_HD)
        proj = jnp.dot(x, w_ref[:, 2 * h * XA_HD:2 * (h + 1) * XA_HD], preferred_element_type=F32)
        qh = proj[:, :XA_HD].astype(BF16)
        path = proj[:, XA_HD:]
        kh = kv_ref[:, sl]
        vh = kv_ref[:, BRANCH_W + h * XA_HD:BRANCH_W + (h + 1) * XA_HD]
        s = lax.dot_general(qh, kh, (((1,), (1,)), ((), ())), preferred_element_type=F32) * scale
        mx = jnp.max(s, axis=-1, keepdims=True)
        p = jnp.exp(s - mx)
        den = jnp.sum(p, axis=-1, keepdims=True)
        o = jnp.dot(p.astype(BF16), vh, preferred_element_type=F32) / den
        o_ref[:, sl] = (o * _silu(path)).astype(BF16)


def cross_attention_proj(xb, w_xa, layer, kv, bsz, seq, mem_len, tq=TQ_XA):
    t, d = xb.shape
    assert seq % tq == 0
    nq = seq // tq
    return pl.pallas_call(
        _xattn_proj_kernel,
        grid=(bsz, nq),
        in_specs=[
            pl.BlockSpec((tq, d), lambda b, i: (b * nq + i, 0)),
            pl.BlockSpec((None,) + w_xa.shape[1:], lambda b, i: (layer, 0, 0), pipeline_mode=pl.Buffered(1)),
            pl.BlockSpec((mem_len, 2 * BRANCH_W), lambda b, i: (b, 0)),
        ],
        out_specs=pl.BlockSpec((tq, BRANCH_W), lambda b, i: (b * nq + i, 0)),
        out_shape=jax.ShapeDtypeStruct((t, BRANCH_W), BF16),
        compiler_params=_cparams(("parallel", "parallel")),
        name="cross_attn_proj",
    )(xb, w_xa, kv)


def _merge_kernel(x_ref, za_ref, zb_ref, zc_ref, zd_ref, ze_ref, wg_ref, wb_ref, o_ref):
    x = x_ref[...]
    acc = None
    for n, z_ref in enumerate((za_ref, zb_ref, zc_ref, zd_ref, ze_ref)):
        gate = _sigmoid(jnp.dot(x, wg_ref[n], preferred_element_type=F32))
        term = gate * jnp.dot(z_ref[...], wb_ref[n].astype(BF16), preferred_element_type=F32)
        acc = term if acc is None else acc + term
    o_ref[...] = acc.astype(BF16)


def gated_merge(xb, zs, wg, wb, layer, tm=TM_MERGE, tn=TN_MERGE):
    t, d = xb.shape
    tm = min(tm, t)
    w = BRANCH_W
    z_spec = pl.BlockSpec((tm, w), lambda i, j: (i, 0))
    return pl.pallas_call(
        _merge_kernel,
        grid=(t // tm, d // tn),
        in_specs=[pl.BlockSpec((tm, d), lambda i, j: (i, 0)),
                  z_spec, z_spec, z_spec, z_spec, z_spec,
                  pl.BlockSpec((None, N_BRANCH, d, tn), lambda i, j: (layer, 0, 0, j)),
                  pl.BlockSpec((None, N_BRANCH, w, tn), lambda i, j: (layer, 0, 0, j))],
        out_specs=pl.BlockSpec((tm, tn), lambda i, j: (i, j)),
        out_shape=jax.ShapeDtypeStruct((t, d), BF16),
        compiler_params=_cparams(("parallel", "parallel")),
        name="gated_merge",
    )(xb, *zs, wg, wb)


def _out_kernel(m_ref, w_ref, x_ref, g_ref, b_ref, of_ref, ob_ref, *, rc):
    w = w_ref[...].astype(BF16)
    for r in range(m_ref.shape[0] // rc):
        rows = slice(r * rc, (r + 1) * rc)
        out = jnp.dot(m_ref[rows, :], w, preferred_element_type=F32)
        y = _ln_rows(DN_ALPHA * x_ref[rows, :] + out, g_ref[...], b_ref[...])
        of_ref[rows, :] = y
        ob_ref[rows, :] = y.astype(BF16)


def out_proj_ln(merged, w_out, layer, x, g, b, tm=TM_OUT):
    t, d = x.shape
    tm = min(tm, t)
    row = pl.BlockSpec((tm, d), lambda i: (i, 0))
    vec = pl.BlockSpec((1, d), lambda i: (0, 0))
    return pl.pallas_call(
        functools.partial(_out_kernel, rc=min(RC_OUT, tm)),
        grid=(t // tm,),
        in_specs=[row,
                  pl.BlockSpec((None, d, d), lambda i: (layer, 0, 0), pipeline_mode=pl.Buffered(1)),
                  row, vec, vec],
        out_specs=[row, row],
        out_shape=[jax.ShapeDtypeStruct((t, d), F32), jax.ShapeDtypeStruct((t, d), BF16)],
        compiler_params=_cparams(("parallel",)),
        name="out_proj_ln",
    )(merged, w_out, x, g.reshape(1, d), b.reshape(1, d))


def _regroup_segments():
    silu = lambda n: O_SILU + n * BRANCH_W
    attn = [(O_AQ, 0, BRANCH_W), (silu(0), BRANCH_W, BRANCH_W), (O_AK, 2 * BRANCH_W, O_SCH - O_AK)]
    conv = []
    for s in range(BRANCH_W // CV_SLAB):
        for gi, o in enumerate((O_SCH, O_SCBG, O_SCCG, O_CFV, O_CFG, silu(1), silu(2))):
            conv.append((o + s * CV_SLAB, (s * CV_GROUPS + gi) * CV_SLAB, CV_SLAB))
    ml = []
    for h in range(ML_HEADS):
        d0 = LANES + h * ML_HEAD_COLS
        ml += [(O_MLQ + h * ML_DK, d0, ML_DK), (O_MLK + h * ML_DK, d0 + ML_DK, ML_DK),
               (O_MLV + h * ML_DV, d0 + 2 * ML_DK, ML_DV), (O_MLO + h * ML_DV, d0 + 2 * ML_DK + ML_DV, ML_DV),
               (silu(3) + h * ML_DV, d0 + 2 * ML_DK + 2 * ML_DV, ML_DV)]
    xa = []
    for h in range(XA_HEADS):
        xa += [(O_XAQ + h * XA_HD, 2 * h * XA_HD, XA_HD), (silu(4) + h * XA_HD, (2 * h + 1) * XA_HD, XA_HD)]
    return attn, conv, ml, xa


def _regroup_kernel(wt_ref, attn_ref, conv_ref, ml_ref, xa_ref):
    for dst, segs in zip((attn_ref, conv_ref, ml_ref, xa_ref), _regroup_segments()):
        for src, d0, width in segs:
            dst[:, d0:d0 + width] = wt_ref[src:src + width, :].T.astype(BF16)
    tile = wt_ref[O_MLI:O_MLI + LANES, :].T
    lane = lax.broadcasted_iota(jnp.int32, tile.shape, 1)
    ml_ref[:, 0:LANES] = jnp.where(lane < 2 * ML_HEADS, tile, 0.0).astype(BF16)


def regroup_w_in(w_in, rb=128):
    depth, d, n_in = w_in.shape
    widths = (2 * BRANCH_W + O_SCH - O_AK, CV_GROUPS * BRANCH_W, LANES + ML_HEADS * ML_HEAD_COLS, 2 * BRANCH_W)
    return pl.pallas_call(
        _regroup_kernel,
        grid=(depth, d // rb),
        in_specs=[pl.BlockSpec((None, n_in, rb), lambda l, r: (l, 0, r))],
        out_specs=[pl.BlockSpec((None, rb, w), lambda l, r: (l, r, 0)) for w in widths],
        out_shape=[jax.ShapeDtypeStruct((depth, d, w), BF16) for w in widths],
        compiler_params=_cparams(("parallel", "parallel")),
        name="regroup_w_in",
    )(jnp.swapaxes(w_in, 1, 2))


def prepare_weights(w_in, w_gate, w_bout, w_out, mem_wkv, ml_i_bias, ml_f_bias):
    attn, conv, ml, xa = regroup_w_in(w_in)
    pad = LANES - 2 * ML_HEADS
    return dict(
        attn=attn, conv=conv, ml=ml, xa=xa,
        bias=jnp.pad(jnp.concatenate([ml_i_bias, ml_f_bias], axis=-1).astype(F32), ((0, 0), (0, pad))),
        kv=mem_wkv,
        wg=w_gate.astype(BF16),
        wb=w_bout,
        wo=w_out,
    )


def hybrid_layer(xf, xb, mem_b, bsz, seq, wts, l, sinks, sc_w, cf_w, cf_g, cf_b, ln_g, ln_b):
    mem_len = mem_b.shape[0] // bsz
    qkvp = matmul(xb, wts["attn"], l, BF16, TM_MM, wts["attn"].shape[-1], "proj_attn")
    kv = matmul(mem_b, wts["kv"], l, BF16, TM_MM, 1024, "mem_kv")

    z_a = window_attention(qkvp, sinks, bsz, seq)
    z_b, z_c = conv_mixers_proj(xb, wts["conv"], l, sc_w, cf_w, cf_g, cf_b, bsz, seq)
    z_d = mlstm_proj(xb, wts["ml"], l, wts["bias"][l].reshape(1, LANES), bsz, seq)
    z_e = cross_attention_proj(xb, wts["xa"], l, kv, bsz, seq, mem_len)

    merged = gated_merge(xb, (z_a, z_b, z_c, z_d, z_e), wts["wg"], wts["wb"], l)
    return out_proj_ln(merged, wts["wo"], l, xf, ln_g, ln_b)


def kernel(x, mem, ln_in_g, ln_in_b, w_in, w_gate, w_bout, w_out, attn_sinks, sc_w, cf_w, cf_ln_g,
           cf_ln_b, ml_i_bias, ml_f_bias, mem_wkv, ln_g, ln_b):
    bsz, seq, d = x.shape
    wts = prepare_weights(w_in, w_gate, w_bout, w_out, mem_wkv, ml_i_bias, ml_f_bias)
    xf, xb = layer_norm_in(x.reshape(bsz * seq, d), ln_in_g, ln_in_b)
    mem_b = mem.reshape(-1, d).astype(BF16)
    for l in range(w_in.shape[0]):
        xf, xb = hybrid_layer(xf, xb, mem_b, bsz, seq, wts, l, attn_sinks[l], sc_w[l], cf_w[l],
                              cf_ln_g[l], cf_ln_b[l], ln_g[l], ln_b[l])
    return xf.reshape(bsz, seq, d)
```

```python
import functools

import jax
import jax.numpy as jnp
import numpy as np
from jax import lax
from jax.experimental import pallas as pl
from jax.experimental.pallas import tpu as pltpu

F32 = jnp.float32
BF16 = jnp.bfloat16

D_MODEL = 2048
DEPTH = 4
CHUNK = 64
BRANCH_W = 1024
N_BRANCH = 5
A_HEADS = 16
A_KV_HEADS = 2
A_HD = 64
A_WINDOW = 128
SC_WIDTH = 3
CF_WIDTH = 31
ML_HEADS = 4
ML_DK = 128
ML_DV = 256
XA_HEADS = 4
XA_HD = BRANCH_W // XA_HEADS
DN_ALPHA = (2 * DEPTH) ** 0.25
LN_EPS = 1e-5

IN_SPLITS = (
    A_HEADS * A_HD, A_KV_HEADS * A_HD, A_KV_HEADS * A_HD,
    BRANCH_W, BRANCH_W, BRANCH_W,
    BRANCH_W, BRANCH_W,
    ML_HEADS * ML_DK, ML_HEADS * ML_DK, ML_HEADS * ML_DV,
    ML_HEADS, ML_HEADS, BRANCH_W,
    BRANCH_W,
    N_BRANCH * BRANCH_W,
)
_OFF = tuple(int(v) for v in np.cumsum((0,) + IN_SPLITS))
(O_AQ, O_AK, O_AV, O_SCH, O_SCBG, O_SCCG, O_CFV, O_CFG, O_MLQ, O_MLK, O_MLV,
 O_MLI, O_MLF, O_MLO, O_XAQ, O_SILU, O_END) = _OFF

LANES = 128
HALO = 32
ALIBI_SLOPES = tuple(float(2.0 ** (-8.0 * (h + 1) / A_HEADS)) for h in range(A_HEADS))

TM_MM = 1024
TQ_ATTN = 2048
TS_CONV = 256
L_MLSTM = 256
TT_MLSTM = 256
TQ_XA = 512
TM_MERGE = 1024
TN_MERGE = 256
TM_OUT = 512
RC_OUT = 256
VMEM_LIMIT = 56 * 1024 * 1024


def _cparams(sem):
    return pltpu.CompilerParams(dimension_semantics=sem, vmem_limit_bytes=VMEM_LIMIT)


def _sigmoid(x):
    return 1.0 / (1.0 + jnp.exp(-x))


def _silu(x):
    return x * _sigmoid(x)


def _ln_rows(x, g, b):
    mu = jnp.mean(x, axis=-1, keepdims=True)
    xc = x - mu
    var = jnp.mean(xc * xc, axis=-1, keepdims=True)
    return xc * lax.rsqrt(var + LN_EPS) * g + b


def _ln_kernel(x_ref, g_ref, b_ref, of_ref, ob_ref):
    y = _ln_rows(x_ref[...], g_ref[...], b_ref[...])
    of_ref[...] = y
    ob_ref[...] = y.astype(BF16)


def layer_norm_in(x2d, g, b, tm=512):
    t, d = x2d.shape
    return pl.pallas_call(
        _ln_kernel,
        grid=(t // tm,),
        in_specs=[pl.BlockSpec((tm, d), lambda i: (i, 0)),
                  pl.BlockSpec((1, d), lambda i: (0, 0)),
                  pl.BlockSpec((1, d), lambda i: (0, 0))],
        out_specs=[pl.BlockSpec((tm, d), lambda i: (i, 0)),
                   pl.BlockSpec((tm, d), lambda i: (i, 0))],
        out_shape=[jax.ShapeDtypeStruct((t, d), F32), jax.ShapeDtypeStruct((t, d), BF16)],
        compiler_params=_cparams(("parallel",)),
        name="ln_in",
    )(x2d, g.reshape(1, d), b.reshape(1, d))


def _mm_kernel(a_ref, b_ref, o_ref):
    o_ref[...] = jnp.dot(a_ref[...], b_ref[...].astype(BF16), preferred_element_type=F32).astype(o_ref.dtype)


def matmul(a, b, layer, out_dtype, tm, tn, name):
    m, k = a.shape
    n = b.shape[2]
    tm = min(tm, m)
    return pl.pallas_call(
        _mm_kernel,
        grid=(m // tm, n // tn),
        in_specs=[pl.BlockSpec((tm, k), lambda i, j: (i, 0)),
                  pl.BlockSpec((None, k, tn), lambda i, j: (layer, 0, j))],
        out_specs=pl.BlockSpec((tm, tn), lambda i, j: (i, j)),
        out_shape=jax.ShapeDtypeStruct((m, n), out_dtype),
        compiler_params=_cparams(("parallel", "parallel")),
        name=name,
    )(a, b)


QB_ATTN = A_WINDOW


def _attn_kernel(q_ref, kvc_ref, kvp_ref, path_ref, snk_ref, bias_ref, o_ref, *, tq):
    i = pl.program_id(1)
    nk = A_WINDOW + tq
    kv = jnp.concatenate([kvp_ref[...], kvc_ref[...]], axis=0)
    low = lax.broadcasted_iota(jnp.int32, (nk, LANES), 1) < A_HD
    low_row = lax.broadcasted_iota(jnp.int32, (LANES, ATTN_PAIRS * QB_ATTN), 0) < A_HD
    zero = jnp.zeros((nk, LANES), BF16)

    def halves(blk):
        swapped = jnp.concatenate([blk[:, A_HD:], blk[:, :A_HD]], axis=1)
        return {(0, 0): jnp.where(low, blk, zero), (0, 1): jnp.where(low, zero, swapped),
                (1, 0): jnp.where(low, swapped, zero), (1, 1): jnp.where(low, zero, blk)}

    k_half = halves(kv[:, :LANES] * (A_HD ** -0.5))
    v_half = halves(kv[:, LANES:])
    for r in range(tq // QB_ATTN):
        rows = slice(r * QB_ATTN, (r + 1) * QB_ATTN)
        keys = slice(r * QB_ATTN, r * QB_ATTN + 2 * A_WINDOW)
        tbl = jnp.where(i == 0, 0, 1) if r == 0 else 1
        for g in range(A_KV_HEADS):
            pairs = range(g * ATTN_PAIRS, (g + 1) * ATTN_PAIRS)
            q_stack = jnp.concatenate([q_ref[rows, hp * LANES:(hp + 1) * LANES] for hp in pairs], axis=0)
            o_t = None
            rden = []
            for par in range(2):
                s = lax.dot_general(k_half[g, par][keys], q_stack, (((1,), (1,)), ((), ())),
                                    preferred_element_type=F32)
                s = s + bias_ref[tbl, g, par]
                snk = snk_ref[2 * g + par:2 * g + par + 1, :]
                mx = jnp.maximum(jnp.max(s, axis=0, keepdims=True), snk)
                p = jnp.exp(s - mx)
                den = jnp.sum(p, axis=0, keepdims=True) + jnp.exp(snk - mx)
                o = lax.dot_general(v_half[g, par][keys], p.astype(BF16), (((0,), (0,)), ((), ())),
                                    preferred_element_type=F32)
                o_t = o if o_t is None else o_t + o
                rden.append(1.0 / den)
            y_t = o_t * jnp.where(low_row, rden[0], rden[1])
            for j, hp in enumerate(pairs):
                y = y_t[:, j * QB_ATTN:(j + 1) * QB_ATTN].T
                path = path_ref[rows, hp * LANES:(hp + 1) * LANES].astype(F32)
                o_ref[rows, hp * LANES:(hp + 1) * LANES] = (y * _silu(path)).astype(BF16)


ATTN_PAIRS = A_HEADS // A_KV_HEADS // 2


def attn_bias_tables():
    t = np.arange(QB_ATTN)[:, None]
    s = np.arange(2 * A_WINDOW)[None, :]
    q_chunk = t // CHUNK + A_WINDOW // CHUNK
    k_chunk = s // CHUNK
    valid = (k_chunk <= q_chunk) & (k_chunk >= q_chunk - A_WINDOW // CHUNK)
    dist = np.abs(t + A_WINDOW - s).astype(np.float32)
    slopes = np.asarray(ALIBI_SLOPES, np.float32)[:, None, None]
    general = np.where(valid[None], -slopes * dist[None], -np.inf)
    first = np.where((valid & (s >= A_WINDOW))[None], -slopes * dist[None], -np.inf)
    per_head = np.stack([first, general]).astype(np.float32)
    tab = per_head.reshape(2, A_KV_HEADS, ATTN_PAIRS, 2, QB_ATTN, 2 * A_WINDOW).transpose(0, 1, 3, 2, 4, 5)
    tab = tab.reshape(2, A_KV_HEADS, 2, ATTN_PAIRS * QB_ATTN, 2 * A_WINDOW)
    return jnp.asarray(np.ascontiguousarray(np.swapaxes(tab, -1, -2)))


def window_attention(qpkv, sinks, bsz, seq, tq=TQ_ATTN):
    t = bsz * seq
    assert seq % tq == 0
    nq = seq // tq
    kvw = 2 * A_KV_HEADS * A_HD
    kv_col = (A_HEADS * A_HD + BRANCH_W) // kvw
    snk = sinks.astype(F32).reshape(A_KV_HEADS, ATTN_PAIRS, 2).transpose(0, 2, 1).reshape(2 * A_KV_HEADS, ATTN_PAIRS)
    snk = jnp.pad(jnp.repeat(snk, QB_ATTN, axis=1), ((0, 8 - 2 * A_KV_HEADS), (0, 0)))
    bias = attn_bias_tables()
    return pl.pallas_call(
        functools.partial(_attn_kernel, tq=tq),
        grid=(bsz, nq),
        in_specs=[
            pl.BlockSpec((tq, A_HEADS * A_HD), lambda b, i: (b * nq + i, 0)),
            pl.BlockSpec((tq, kvw), lambda b, i: (b * nq + i, kv_col)),
            pl.BlockSpec((A_WINDOW, kvw),
                         lambda b, i: (jnp.maximum((b * nq + i) * (tq // A_WINDOW) - 1, 0), kv_col)),
            pl.BlockSpec((tq, BRANCH_W), lambda b, i: (b * nq + i, 1)),
            pl.BlockSpec(snk.shape, lambda b, i: (0, 0)),
            pl.BlockSpec(bias.shape, lambda b, i: (0,) * bias.ndim),
        ],
        out_specs=pl.BlockSpec((tq, BRANCH_W), lambda b, i: (b * nq + i, 0)),
        out_shape=jax.ShapeDtypeStruct((t, BRANCH_W), BF16),
        compiler_params=_cparams(("parallel", "parallel")),
        name="window_attn",
    )(qpkv, qpkv, qpkv, qpkv, snk, bias)


CV_SLAB = 256
CV_GROUPS = 7


def _conv_proj_kernel(x_ref, w_ref, scw_ref, cfw_ref, lng_ref, lnb_ref, zb_ref, zc_ref,
                      ub_scr, uc_scr, cv_scr, pc_scr, *, ts):
    i = pl.program_id(1)
    n_sub = BRANCH_W // LANES

    @pl.when(i == 0)
    def _():
        ub_scr[:, 0:HALO, :] = jnp.zeros((n_sub, HALO, LANES), F32)
        uc_scr[:, 0:HALO, :] = jnp.zeros((n_sub, HALO, LANES), F32)

    x = x_ref[...]
    for s4 in range(BRANCH_W // CV_SLAB):
        c0 = s4 * CV_GROUPS * CV_SLAB
        proj = jnp.dot(x, w_ref[:, c0:c0 + CV_GROUPS * CV_SLAB], preferred_element_type=F32)
        for half in range(CV_SLAB // LANES):
            s = s4 * (CV_SLAB // LANES) + half
            ln = slice(s * LANES, (s + 1) * LANES)

            def grp(g, half=half, proj=proj):
                return proj[:, g * CV_SLAB + half * LANES:g * CV_SLAB + (half + 1) * LANES]

            ub_scr[s, HALO:HALO + ts, :] = grp(2) * grp(0)
            uc_scr[s, HALO:HALO + ts, :] = grp(3) * _sigmoid(grp(4))
            acc = jnp.zeros((ts, LANES), F32)
            for j in range(SC_WIDTH):
                win = ub_scr[s, pl.ds(HALO - (SC_WIDTH - 1) + j, ts, stride=1), :]
                acc = acc + scw_ref[j:j + 1, ln] * win
            zb_ref[:, ln] = (grp(1) * acc * _silu(grp(5))).astype(BF16)
            acc = jnp.zeros((ts, LANES), F32)
            for j in range(CF_WIDTH):
                win = uc_scr[s, pl.ds(HALO - (CF_WIDTH - 1) + j, ts, stride=1), :]
                acc = acc + cfw_ref[j:j + 1, ln] * win
            cv_scr[:, ln] = acc
            pc_scr[:, ln] = _silu(grp(6))
            ub_scr[s, 0:HALO, :] = ub_scr[s, ts:ts + HALO, :]
            uc_scr[s, 0:HALO, :] = uc_scr[s, ts:ts + HALO, :]

    yc = _silu(_ln_rows(cv_scr[...], lng_ref[...], lnb_ref[...]))
    zc_ref[...] = (yc * pc_scr[...]).astype(BF16)


def conv_mixers_proj(xb, w_cv, layer, sc_w, cf_w, cf_g, cf_b, bsz, seq, ts=TS_CONV):
    t, d = xb.shape
    assert seq % ts == 0
    ns = seq // ts
    w = BRANCH_W
    n_sub = w // LANES

    def full(shape):
        return pl.BlockSpec(shape, lambda b, i: (0, 0))

    out_spec = pl.BlockSpec((ts, w), lambda b, i: (b * ns + i, 0))
    return pl.pallas_call(
        functools.partial(_conv_proj_kernel, ts=ts),
        grid=(bsz, ns),
        in_specs=[pl.BlockSpec((ts, d), lambda b, i: (b * ns + i, 0)),
                  pl.BlockSpec((None,) + w_cv.shape[1:], lambda b, i: (layer, 0, 0),
                               pipeline_mode=pl.Buffered(1)),
                  full((SC_WIDTH, w)), full((CF_WIDTH, w)), full((1, w)), full((1, w))],
        out_specs=[out_spec, out_spec],
        out_shape=[jax.ShapeDtypeStruct((t, w), BF16), jax.ShapeDtypeStruct((t, w), BF16)],
        scratch_shapes=[pltpu.VMEM((n_sub, HALO + ts, LANES), F32),
                        pltpu.VMEM((n_sub, HALO + ts, LANES), F32),
                        pltpu.VMEM((ts, w), F32), pltpu.VMEM((ts, w), F32)],
        compiler_params=_cparams(("parallel", "arbitrary")),
        name="conv_mixers_proj",
    )(xb, w_cv, sc_w.astype(F32), cf_w.astype(F32), cf_g.reshape(1, w).astype(F32), cf_b.reshape(1, w).astype(F32))


def _log_sigmoid(x):
    return jnp.minimum(x, 0.0) - jnp.log(1.0 + jnp.exp(-jnp.abs(x)))


ML_HEAD_COLS = 2 * ML_DK + 3 * ML_DV


def _mlstm_proj_kernel(x_ref, w_ref, bias_ref, z_ref, c_scr, m_scr, *, tt, lc):
    c = pl.program_id(1)

    @pl.when(c == 0)
    def _():
        c_scr[...] = jnp.zeros_like(c_scr)
        m_scr[...] = jnp.zeros_like(m_scr)

    x = x_ref[...]
    gates_all = jnp.dot(x, w_ref[:, 0:LANES], preferred_element_type=F32) + bias_ref[...]
    row = lax.broadcasted_iota(jnp.int32, (lc, lc), 0)
    col = lax.broadcasted_iota(jnp.int32, (lc, lc), 1)
    causal = col <= row
    tri = causal.astype(F32)
    n_chunks = tt // lc
    per_chunk = []
    for ci in range(n_chunks):
        gates = gates_all[ci * lc:(ci + 1) * lc]
        bcum = jnp.dot(tri, _log_sigmoid(gates), preferred_element_type=F32,
                       precision=lax.Precision.HIGHEST)
        per_chunk.append((gates, gates.T, bcum, bcum.T))
    ones = jnp.ones((lc, LANES), BF16)
    scale = ML_DK ** -0.5
    for h in range(ML_HEADS):
        c0 = LANES + h * ML_HEAD_COLS
        proj = jnp.dot(x, w_ref[:, c0:c0 + ML_HEAD_COLS], preferred_element_type=F32)
        fl = ML_HEADS + h
        for ci in range(n_chunks):
            rows = slice(ci * lc, (ci + 1) * lc)
            gates, gates_t, bcum, bcum_t = per_chunk[ci]
            b_col = bcum[:, fl:fl + 1]
            b_row = bcum_t[fl:fl + 1, :]
            i_col = gates[:, h:h + 1]
            i_row = gates_t[h:h + 1, :]
            m_prev = m_scr[h:h + 1, 0:1]
            dmat = jnp.where(causal, b_col - b_row + i_row, -jnp.inf)
            inter = b_col + m_prev
            m_t = jnp.maximum(inter, jnp.max(dmat, axis=-1, keepdims=True))
            w_intra = jnp.exp(dmat - m_t)
            w_inter = jnp.exp(inter - m_t)
            qh = proj[rows, 0:ML_DK].astype(BF16)
            kh = proj[rows, ML_DK:2 * ML_DK]
            vh = proj[rows, 2 * ML_DK:2 * ML_DK + ML_DV].astype(BF16)
            og = proj[rows, 2 * ML_DK + ML_DV:2 * ML_DK + 2 * ML_DV]
            path = proj[rows, 2 * ML_DK + 2 * ML_DV:ML_HEAD_COLS]
            v_ext = jnp.concatenate([vh, ones], axis=1)
            qk = lax.dot_general(qh, kh.astype(BF16), (((1,), (1,)), ((), ())), preferred_element_type=F32)
            qk = qk * scale * w_intra
            state = c_scr[h]
            num_ext = (jnp.dot(qk.astype(BF16), v_ext, preferred_element_type=F32)
                       + w_inter * jnp.dot(qh, state.astype(BF16), preferred_element_type=F32))
            num = num_ext[:, :ML_DV]
            den = num_ext[:, ML_DV:]
            den = jnp.maximum(jnp.abs(den), jnp.exp(-m_t))
            hh = num / jnp.concatenate([den, den], axis=1)
            b_last = bcum[lc - 1:lc, fl:fl + 1]
            m_new = m_t[lc - 1:lc, :]
            w_state = jnp.exp(b_last - b_col + i_col - m_new)
            decay = jnp.exp(b_last + m_prev - m_new)
            kw_t = (kh * (scale * w_state)).T.astype(BF16)
            c_scr[h] = decay * state + jnp.dot(kw_t, v_ext, preferred_element_type=F32)
            m_scr[h:h + 1, :] = jnp.broadcast_to(m_new, (1, LANES))
            z_ref[rows, h * ML_DV:(h + 1) * ML_DV] = (_sigmoid(og) * hh * _silu(path)).astype(BF16)


def mlstm_proj(xb, w_ml, layer, bias_row, bsz, seq, tt=TT_MLSTM, lc=L_MLSTM):
    t, d = xb.shape
    assert seq % tt == 0 and tt % lc == 0
    nt = seq // tt
    return pl.pallas_call(
        functools.partial(_mlstm_proj_kernel, tt=tt, lc=lc),
        grid=(bsz, nt),
        in_specs=[
            pl.BlockSpec((tt, d), lambda b, c: (b * nt + c, 0)),
            pl.BlockSpec((None,) + w_ml.shape[1:], lambda b, c: (layer, 0, 0), pipeline_mode=pl.Buffered(1)),
            pl.BlockSpec((1, LANES), lambda b, c: (0, 0)),
        ],
        out_specs=pl.BlockSpec((tt, BRANCH_W), lambda b, c: (b * nt + c, 0)),
        out_shape=jax.ShapeDtypeStruct((t, BRANCH_W), BF16),
        scratch_shapes=[pltpu.VMEM((ML_HEADS, ML_DK, ML_DV + LANES), F32),
                        pltpu.VMEM((8, LANES), F32)],
        compiler_params=_cparams(("parallel", "arbitrary")),
        name="mlstm_proj",
    )(xb, w_ml, bias_row)


def _xattn_proj_kernel(x_ref, w_ref, kv_ref, o_ref):
    scale = XA_HD ** -0.5
    x = x_ref[...]
    for h in range(XA_HEADS):
        sl = slice(h * XA_HD, (h + 1) * XA_HD)
        proj = jnp.dot(x, w_ref[:, 2 * h * XA_HD:2 * (h + 1) * XA_HD], preferred_element_type=F32)
        qh = proj[:, :XA_HD].astype(BF16)
        path = proj[:, XA_HD:]
        kh = kv_ref[:, sl]
        vh = kv_ref[:, BRANCH_W + h * XA_HD:BRANCH_W + (h + 1) * XA_HD]
        s = lax.dot_general(qh, kh, (((1,), (1,)), ((), ())), preferred_element_type=F32) * scale
        mx = jnp.max(s, axis=-1, keepdims=True)
        p = jnp.exp(s - mx)
        den = jnp.sum(p, axis=-1, keepdims=True)
        o = jnp.dot(p.astype(BF16), vh, preferred_element_type=F32) / den
        o_ref[:, sl] = (o * _silu(path)).astype(BF16)


def cross_attention_proj(xb, w_xa, layer, kv, bsz, seq, mem_len, tq=TQ_XA):
    t, d = xb.shape
    assert seq % tq == 0
    nq = seq // tq
    return pl.pallas_call(
        _xattn_proj_kernel,
        grid=(bsz, nq),
        in_specs=[
            pl.BlockSpec((tq, d), lambda b, i: (b * nq + i, 0)),
            pl.BlockSpec((None,) + w_xa.shape[1:], lambda b, i: (layer, 0, 0), pipeline_mode=pl.Buffered(1)),
            pl.BlockSpec((mem_len, 2 * BRANCH_W), lambda b, i: (b, 0)),
        ],
        out_specs=pl.BlockSpec((tq, BRANCH_W), lambda b, i: (b * nq + i, 0)),
        out_shape=jax.ShapeDtypeStruct((t, BRANCH_W), BF16),
        compiler_params=_cparams(("parallel", "parallel")),
        name="cross_attn_proj",
    )(xb, w_xa, kv)


def _merge_kernel(x_ref, za_ref, zb_ref, zc_ref, zd_ref, ze_ref, wg_ref, wb_ref, o_ref):
    x = x_ref[...]
    acc = None
    for n, z_ref in enumerate((za_ref, zb_ref, zc_ref, zd_ref, ze_ref)):
        gate = _sigmoid(jnp.dot(x, wg_ref[n], preferred_element_type=F32))
        term = gate * jnp.dot(z_ref[...], wb_ref[n].astype(BF16), preferred_element_type=F32)
        acc = term if acc is None else acc + term
    o_ref[...] = acc.astype(BF16)


def gated_merge(xb, zs, wg, wb, layer, tm=TM_MERGE, tn=TN_MERGE):
    t, d = xb.shape
    tm = min(tm, t)
    w = BRANCH_W
    z_spec = pl.BlockSpec((tm, w), lambda i, j: (i, 0))
    return pl.pallas_call(
        _merge_kernel,
        grid=(t // tm, d // tn),
        in_specs=[pl.BlockSpec((tm, d), lambda i, j: (i, 0)),
                  z_spec, z_spec, z_spec, z_spec, z_spec,
                  pl.BlockSpec((None, N_BRANCH, d, tn), lambda i, j: (layer, 0, 0, j)),
                  pl.BlockSpec((None, N_BRANCH, w, tn), lambda i, j: (layer, 0, 0, j))],
        out_specs=pl.BlockSpec((tm, tn), lambda i, j: (i, j)),
        out_shape=jax.ShapeDtypeStruct((t, d), BF16),
        compiler_params=_cparams(("parallel", "parallel")),
        name="gated_merge",
    )(xb, *zs, wg, wb)


def _out_kernel(m_ref, w_ref, x_ref, g_ref, b_ref, of_ref, ob_ref, *, rc):
    w = w_ref[...].astype(BF16)
    for r in range(m_ref.shape[0] // rc):
        rows = slice(r * rc, (r + 1) * rc)
        out = jnp.dot(m_ref[rows, :], w, preferred_element_type=F32)
        y = _ln_rows(DN_ALPHA * x_ref[rows, :] + out, g_ref[...], b_ref[...])
        of_ref[rows, :] = y
        ob_ref[rows, :] = y.astype(BF16)


def out_proj_ln(merged, w_out, layer, x, g, b, tm=TM_OUT):
    t, d = x.shape
    tm = min(tm, t)
    row = pl.BlockSpec((tm, d), lambda i: (i, 0))
    vec = pl.BlockSpec((1, d), lambda i: (0, 0))
    return pl.pallas_call(
        functools.partial(_out_kernel, rc=min(RC_OUT, tm)),
        grid=(t // tm,),
        in_specs=[row,
                  pl.BlockSpec((None, d, d), lambda i: (layer, 0, 0), pipeline_mode=pl.Buffered(1)),
                  row, vec, vec],
        out_specs=[row, row],
        out_shape=[jax.ShapeDtypeStruct((t, d), F32), jax.ShapeDtypeStruct((t, d), BF16)],
        compiler_params=_cparams(("parallel",)),
        name="out_proj_ln",
    )(merged, w_out, x, g.reshape(1, d), b.reshape(1, d))


def _regroup_segments():
    silu = lambda n: O_SILU + n * BRANCH_W
    attn = [(O_AQ, 0, BRANCH_W), (silu(0), BRANCH_W, BRANCH_W), (O_AK, 2 * BRANCH_W, O_SCH - O_AK)]
    conv = []
    for s in range(BRANCH_W // CV_SLAB):
        for gi, o in enumerate((O_SCH, O_SCBG, O_SCCG, O_CFV, O_CFG, silu(1), silu(2))):
            conv.append((o + s * CV_SLAB, (s * CV_GROUPS + gi) * CV_SLAB, CV_SLAB))
    ml = []
    for h in range(ML_HEADS):
        d0 = LANES + h * ML_HEAD_COLS
        ml += [(O_MLQ + h * ML_DK, d0, ML_DK), (O_MLK + h * ML_DK, d0 + ML_DK, ML_DK),
               (O_MLV + h * ML_DV, d0 + 2 * ML_DK, ML_DV), (O_MLO + h * ML_DV, d0 + 2 * ML_DK + ML_DV, ML_DV),
               (silu(3) + h * ML_DV, d0 + 2 * ML_DK + 2 * ML_DV, ML_DV)]
    xa = []
    for h in range(XA_HEADS):
        xa += [(O_XAQ + h * XA_HD, 2 * h * XA_HD, XA_HD), (silu(4) + h * XA_HD, (2 * h + 1) * XA_HD, XA_HD)]
    return attn, conv, ml, xa


def _regroup_kernel(wt_ref, attn_ref, conv_ref, ml_ref, xa_ref):
    for dst, segs in zip((attn_ref, conv_ref, ml_ref, xa_ref), _regroup_segments()):
        for src, d0, width in segs:
            dst[:, d0:d0 + width] = wt_ref[src:src + width, :].T.astype(BF16)
    tile = wt_ref[O_MLI:O_MLI + LANES, :].T
    lane = lax.broadcasted_iota(jnp.int32, tile.shape, 1)
    ml_ref[:, 0:LANES] = jnp.where(lane < 2 * ML_HEADS, tile, 0.0).astype(BF16)


def regroup_w_in(w_in, rb=128):
    depth, d, n_in = w_in.shape
    widths = (2 * BRANCH_W + O_SCH - O_AK, CV_GROUPS * BRANCH_W, LANES + ML_HEADS * ML_HEAD_COLS, 2 * BRANCH_W)
    return pl.pallas_call(
        _regroup_kernel,
        grid=(depth, d // rb),
        in_specs=[pl.BlockSpec((None, n_in, rb), lambda l, r: (l, 0, r))],
        out_specs=[pl.BlockSpec((None, rb, w), lambda l, r: (l, r, 0)) for w in widths],
        out_shape=[jax.ShapeDtypeStruct((depth, d, w), BF16) for w in widths],
        compiler_params=_cparams(("parallel", "parallel")),
        name="regroup_w_in",
    )(jnp.swapaxes(w_in, 1, 2))


def prepare_weights(w_in, w_gate, w_bout, w_out, mem_wkv, ml_i_bias, ml_f_bias):
    attn, conv, ml, xa = regroup_w_in(w_in)
    pad = LANES - 2 * ML_HEADS
    return dict(
        attn=attn, conv=conv, ml=ml, xa=xa,
        bias=jnp.pad(jnp.concatenate([ml_i_bias, ml_f_bias], axis=-1).astype(F32), ((0, 0), (0, pad))),
        kv=mem_wkv,
        wg=w_gate.astype(BF16),
        wb=w_bout,
        wo=w_out,
    )


def hybrid_layer(xf, xb, mem_b, bsz, seq, wts, l, sinks, sc_w, cf_w, cf_g, cf_b, ln_g, ln_b):
    mem_len = mem_b.shape[0] // bsz
    qkvp = matmul(xb, wts["attn"], l, BF16, TM_MM, wts["attn"].shape[-1], "proj_attn")
    kv = matmul(mem_b, wts["kv"], l, BF16, TM_MM, 1024, "mem_kv")

    z_a = window_attention(qkvp, sinks, bsz, seq)
    z_b, z_c = conv_mixers_proj(xb, wts["conv"], l, sc_w, cf_w, cf_g, cf_b, bsz, seq)
    z_d = mlstm_proj(xb, wts["ml"], l, wts["bias"][l].reshape(1, LANES), bsz, seq)
    z_e = cross_attention_proj(xb, wts["xa"], l, kv, bsz, seq, mem_len)

    merged = gated_merge(xb, (z_a, z_b, z_c, z_d, z_e), wts["wg"], wts["wb"], l)
    return out_proj_ln(merged, wts["wo"], l, xf, ln_g, ln_b)


def kernel(x, mem, ln_in_g, ln_in_b, w_in, w_gate, w_bout, w_out, attn_sinks, sc_w, cf_w, cf_ln_g,
           cf_ln_b, ml_i_bias, ml_f_bias, mem_wkv, ln_g, ln_b):
    bsz, seq, d = x.shape
    wts = prepare_weights(w_in, w_gate, w_bout, w_out, mem_wkv, ml_i_bias, ml_f_bias)
    xf, xb = layer_norm_in(x.reshape(bsz * seq, d), ln_in_g, ln_in_b)
    mem_b = mem.reshape(-1, d).astype(BF16)
    for l in range(w_in.shape[0]):
        xf, xb = hybrid_layer(xf, xb, mem_b, bsz, seq, wts, l, attn_sinks[l], sc_w[l], cf_w[l],
                              cf_ln_g[l], cf_ln_b[l], ln_g[l], ln_b[l])
    return xf.reshape(bsz, seq, d)
```
